```python
import jax, jax.numpy as jnp
from jax import lax
import numpy as np

D_MODEL = 1024
BATCH = 16
SEQ = 4096
DEPTH = 1

GRID_W = 64
CTX_LEN = 256
MIX_WIDTH = D_MODEL
RWKV_WIDTH = MIX_WIDTH // 2
HEAD_SIZE = 64
RWKV_HEADS = RWKV_WIDTH // HEAD_SIZE
FNET_WIDTH = MIX_WIDTH - RWKV_WIDTH
FNET_GROUPS = 8
FNET_GROUP_W = FNET_WIDTH // FNET_GROUPS
DECAY_LORA = 64
ICLR_LORA = 64
GATE_LORA = 128
N_DIR = 2
SHIFT_WIDTH = 3 * RWKV_WIDTH + N_DIR * (DECAY_LORA + ICLR_LORA + GATE_LORA)
IN_WIDTH = SHIFT_WIDTH + FNET_WIDTH
N_EXPERTS = 32
TOP_K = 4
D_FF = D_MODEL
SWIGLU_LIMIT = 7.0
SWIGLU_ALPHA = 1.702
EXPERT_BLOCK = 128
DEEPNORM_ALPHA = (2.0 * DEPTH) ** 0.25
DEEPNORM_BETA = (8.0 * DEPTH) ** -0.25
LN_EPS = 1e-5
GN_EPS = 64e-5

kernel_name = "hybrid_rwkv7_fnet_moe_prefix_dit_layer"


def layer_norm(u, g, b):
    u32 = u.astype(jnp.float32)
    mean = jnp.mean(u32, -1, keepdims=True)
    var = jnp.mean(jnp.square(u32 - mean), -1, keepdims=True)
    return ((u32 - mean) * lax.rsqrt(var + LN_EPS) * g + b).astype(u.dtype)


def grid_shift(u):
    b_, t, ch = u.shape
    rows = t // GRID_W
    g = u.reshape(b_, rows, GRID_W, ch)
    p = jnp.pad(g, ((0, 0), (1, 1), (1, 1), (0, 0)))
    nb = (p[:, :-2, 1:-1] + p[:, 2:, 1:-1] + p[:, 1:-1, :-2] + p[:, 1:-1, 2:]) * 0.25
    return nb.reshape(b_, t, ch)


def seq_shift(u):
    p = jnp.pad(u, ((0, 0), (1, 1), (0, 0)))
    return (p[:, :-2] + p[:, 2:]) * 0.5


def heads(z):
    return z.reshape(z.shape[:-1] + (RWKV_HEADS, HEAD_SIZE))


def rwkv_prepare(m, w0, w2_decay, a0, a2_iclr, g2_gate, k_k, k_a):
    b_, t, _ = m.shape
    c = RWKV_WIDTH
    r = m[..., :c]
    k = m[..., c:2 * c]
    v = m[..., 2 * c:3 * c]
    o = 3 * c
    wd = m[..., o:o + N_DIR * DECAY_LORA].reshape(b_, t, N_DIR, DECAY_LORA)
    o += N_DIR * DECAY_LORA
    ad = m[..., o:o + N_DIR * ICLR_LORA].reshape(b_, t, N_DIR, ICLR_LORA)
    o += N_DIR * ICLR_LORA
    gd = m[..., o:o + N_DIR * GATE_LORA].reshape(b_, t, N_DIR, GATE_LORA)
    w_log = w0 + jnp.einsum('btdr,drc->btdc', jnp.tanh(wd), w2_decay)
    w = jnp.exp(-jnp.exp(-jax.nn.softplus(-w_log) - 0.5))
    a = jax.nn.sigmoid(a0 + jnp.einsum('btdr,drc->btdc', ad, a2_iclr))
    g = jnp.einsum('btdr,drc->btdc', jax.nn.sigmoid(gd), g2_gate)
    kk = heads(k * k_k)
    kk = kk / jnp.maximum(jnp.sqrt(jnp.sum(jnp.square(kk), -1, keepdims=True)), 1e-12)
    k_d = k[:, :, None] * (1.0 + (a - 1.0) * k_a)
    return heads(r), heads(k_d), heads(v), kk, heads(w), heads(a), g


def wkv_bidir(r, k_d, v, kk, w, a, state0):
    def both(u):
        return jnp.stack([u, jnp.flip(u, 1)], 0)

    def per(u):
        return jnp.stack([u[:, :, 0], jnp.flip(u[:, :, 1], 1)], 0)

    def tm(z):
        return jnp.moveaxis(z, 2, 0)

    xs = (tm(both(r)), tm(per(w)), tm(per(k_d)), tm(both(v)), tm(both(-kk)), tm(per(kk[:, :, None] * a)))

    def step(s, inp):
        r_t, w_t, k_t, v_t, aa_t, bb_t = inp
        sa = jnp.einsum('dbhij,dbhj->dbhi', s, aa_t)
        s = s * w_t[..., None, :] + sa[..., :, None] * bb_t[..., None, :] + v_t[..., :, None] * k_t[..., None, :]
        y_t = jnp.einsum('dbhij,dbhj->dbhi', s, r_t)
        return s, y_t

    state, ys = lax.scan(step, state0, xs)
    ys = jnp.moveaxis(ys, 0, 2)
    y = jnp.stack([ys[0], jnp.flip(ys[1], 1)], axis=2)
    return y, state


def rwkv_output(y, r, k_d, v, g, r_k, gn_g, gn_b):
    b_, t = y.shape[:2]
    mean = jnp.mean(y, -1, keepdims=True)
    var = jnp.mean(jnp.square(y - mean), -1, keepdims=True)
    yn = (y - mean) * lax.rsqrt(var + GN_EPS) * heads(gn_g) + heads(gn_b)
    bonus = jnp.sum(r[:, :, None] * k_d * r_k, -1, keepdims=True) * v[:, :, None]
    return jnp.sum((yn + bonus).reshape(b_, t, N_DIR, RWKV_WIDTH) * g, axis=2)


def fourier_mix(f, w_fno, b_fno):
    b_, t, _ = f.shape
    fg = f.reshape(b_, t, FNET_GROUPS, FNET_GROUP_W).astype(jnp.float32)
    spec = jnp.real(jnp.fft.fft2(fg, axes=(1, 3), norm='ortho'))
    y = jnp.einsum('btgc,gce->btge', spec, w_fno) + b_fno
    return y.reshape(b_, t, FNET_WIDTH)


def token_mixer(h_lat, h_ctx, w_in, mu_shift, w0, w2_decay, a0, a2_iclr, g2_gate, r_k, k_k, k_a,
                gn_g, gn_b, w_fno, b_fno, w_out, need_ctx):
    f32 = jnp.float32
    p_lat = jnp.einsum('btd,de->bte', h_lat, w_in).astype(f32)
    p_ctx = jnp.einsum('btd,de->bte', h_ctx, w_in).astype(f32)
    s_lat, f_lat = p_lat[..., :SHIFT_WIDTH], p_lat[..., SHIFT_WIDTH:]
    s_ctx, f_ctx = p_ctx[..., :SHIFT_WIDTH], p_ctx[..., SHIFT_WIDTH:]
    mu = mu_shift.astype(f32)
    m_lat = s_lat + mu * (grid_shift(s_lat) - s_lat)
    m_ctx = s_ctx + mu * (seq_shift(s_ctx) - s_ctx)
    dp = (w0, w2_decay, a0, a2_iclr, g2_gate, k_k, k_a)
    r_c, k_c, v_c, kk_c, w_c, a_c, g_c = rwkv_prepare(m_ctx, *dp)
    state0 = jnp.zeros((N_DIR, h_ctx.shape[0], RWKV_HEADS, HEAD_SIZE, HEAD_SIZE), f32)
    y_c, state_c = wkv_bidir(r_c, k_c, v_c, kk_c, w_c, a_c, state0)
    r_l, k_l, v_l, kk_l, w_l, a_l, g_l = rwkv_prepare(m_lat, *dp)
    y_l, _ = wkv_bidir(r_l, k_l, v_l, kk_l, w_l, a_l, state_c)
    rw_lat = rwkv_output(y_l, r_l, k_l, v_l, g_l, r_k, gn_g, gn_b)
    out_lat = jnp.concatenate([rw_lat, fourier_mix(f_lat, w_fno, b_fno)], -1) @ w_out.astype(f32)
    out_ctx = None
    if need_ctx:
        rw_ctx = rwkv_output(y_c, r_c, k_c, v_c, g_c, r_k, gn_g, gn_b)
        out_ctx = (jnp.concatenate([rw_ctx, fourier_mix(f_ctx, w_fno, b_fno)], -1) @ w_out.astype(f32)).astype(h_ctx.dtype)
    return out_lat.astype(h_lat.dtype), out_ctx


def moe_ffn(h, w_router, b_router, w1, b1, w2, b2):
    f32 = jnp.float32
    n_tok = h.shape[0]
    logits = h.astype(f32) @ w_router.astype(f32) + b_router.astype(f32)
    top_val, top_idx = lax.top_k(logits, TOP_K)
    gates = jax.nn.softmax(top_val, axis=-1)
    n_assign = n_tok * TOP_K
    e_flat = top_idx.reshape(-1).astype(jnp.int32)
    tok_flat = jnp.repeat(jnp.arange(n_tok, dtype=jnp.int32), TOP_K)
    g_flat = gates.reshape(-1)
    order = jnp.argsort(e_flat)
    e_s, tok_s, g_s = e_flat[order], tok_flat[order], g_flat[order]
    counts = jnp.bincount(e_flat, length=N_EXPERTS).astype(jnp.int32)
    starts = jnp.cumsum(counts) - counts
    padded = (counts + EXPERT_BLOCK - 1) // EXPERT_BLOCK * EXPERT_BLOCK
    pends = jnp.cumsum(padded)
    pstarts = pends - padded
    slot = pstarts[e_s] + jnp.arange(n_assign, dtype=jnp.int32) - starts[e_s]
    n_blocks = -(-n_assign // EXPERT_BLOCK) + N_EXPERTS
    n_slots = n_blocks * EXPERT_BLOCK
    slot_tok = jnp.zeros((n_slots,), jnp.int32).at[slot].set(tok_s)
    slot_gate = jnp.zeros((n_slots,), f32).at[slot].set(g_s)
    block_start = jnp.arange(n_blocks, dtype=jnp.int32) * EXPERT_BLOCK
    block_e = jnp.minimum(jnp.searchsorted(pends, block_start, side='right'), N_EXPERTS - 1)

    def expert_block(args):
        tok_b, e = args
        xb = h[tok_b]
        u = xb @ w1[e] + b1[e]
        glu = jnp.minimum(u[:, 0::2], SWIGLU_LIMIT)
        lin = jnp.clip(u[:, 1::2], -SWIGLU_LIMIT, SWIGLU_LIMIT)
        act = glu * jax.nn.sigmoid(SWIGLU_ALPHA * glu) * (lin + 1.0)
        return act @ w2[e] + b2[e]

    y = lax.map(expert_block, (slot_tok.reshape(n_blocks, EXPERT_BLOCK), block_e))
    out = jnp.zeros((n_tok, h.shape[1]), f32).at[slot_tok].add(slot_gate[:, None] * y.reshape(n_slots, -1))
    return out.astype(h.dtype)


def setup_inputs(seed: int = 0) -> dict:
    key = jax.random.key(seed)
    ks = iter(jax.random.split(key, 48))
    f32 = jnp.float32

    def nrm(shape, scale):
        return scale * jax.random.normal(next(ks), shape, f32)

    L, C, H, N = DEPTH, RWKV_WIDTH, RWKV_HEADS, HEAD_SIZE
    x = nrm((BATCH, SEQ, D_MODEL), 1.0)
    c = nrm((BATCH, D_MODEL), 1.0)
    ctx = nrm((BATCH, CTX_LEN, D_MODEL), 1.0)
    c_ctx = nrm((D_MODEL,), 1.0)
    ln0_g = 1.0 + nrm((D_MODEL,), 0.05)
    ln0_b = nrm((D_MODEL,), 0.02)
    w_ada = nrm((L, D_MODEL, 6 * D_MODEL), 0.5 * D_MODEL ** -0.5)
    b_ada = nrm((L, 6 * D_MODEL), 0.02)
    w_in = nrm((L, D_MODEL, IN_WIDTH), D_MODEL ** -0.5)
    w_in = w_in.at[:, :, 2 * C:3 * C].multiply(DEEPNORM_BETA)
    mu_shift = jax.random.uniform(next(ks), (L, SHIFT_WIDTH), f32, 0.2, 0.8)
    w0 = jax.random.uniform(next(ks), (L, N_DIR, C), f32, -6.0, 1.0)
    w2_decay = nrm((L, N_DIR, DECAY_LORA, C), 0.5 * DECAY_LORA ** -0.5)
    a0 = nrm((L, N_DIR, C), 0.5)
    a2_iclr = nrm((L, N_DIR, ICLR_LORA, C), 0.5 * ICLR_LORA ** -0.5)
    g2_gate = nrm((L, N_DIR, GATE_LORA, C), GATE_LORA ** -0.5)
    r_k = nrm((L, N_DIR, H, N), 0.1)
    k_k = 0.85 + nrm((L, C), 0.05)
    k_a = 1.0 + nrm((L, C), 0.05)
    gn_g = 1.0 + nrm((L, C), 0.05)
    gn_b = nrm((L, C), 0.02)
    w_fno = nrm((L, FNET_GROUPS, FNET_GROUP_W, FNET_GROUP_W), FNET_GROUP_W ** -0.5)
    b_fno = nrm((L, FNET_GROUPS, FNET_GROUP_W), 0.02)
    w_out = nrm((L, MIX_WIDTH, D_MODEL), DEEPNORM_BETA * MIX_WIDTH ** -0.5)
    ln1_g = 1.0 + nrm((L, D_MODEL), 0.05)
    ln1_b = nrm((L, D_MODEL), 0.02)
    w_router = nrm((L, D_MODEL, N_EXPERTS), D_MODEL ** -0.5)
    b_router = nrm((L, N_EXPERTS), 0.01)
    w1 = nrm((L, N_EXPERTS, D_MODEL, 2 * D_FF), D_MODEL ** -0.5)
    b1 = nrm((L, N_EXPERTS, 2 * D_FF), 0.02)
    w2 = nrm((L, N_EXPERTS, D_FF, D_MODEL), DEEPNORM_BETA * D_FF ** -0.5)
    b2 = nrm((L, N_EXPERTS, D_MODEL), 0.02)
    ln2_g = 1.0 + nrm((L, D_MODEL), 0.05)
    ln2_b = nrm((L, D_MODEL), 0.02)
    return {"x": x, "c": c, "ctx": ctx, "c_ctx": c_ctx, "ln0_g": ln0_g, "ln0_b": ln0_b,
            "w_ada": w_ada, "b_ada": b_ada, "w_in": w_in, "mu_shift": mu_shift, "w0": w0,
            "w2_decay": w2_decay, "a0": a0, "a2_iclr": a2_iclr, "g2_gate": g2_gate, "r_k": r_k,
            "k_k": k_k, "k_a": k_a, "gn_g": gn_g, "gn_b": gn_b, "w_fno": w_fno, "b_fno": b_fno,
            "w_out": w_out, "ln1_g": ln1_g, "ln1_b": ln1_b, "w_router": w_router, "b_router": b_router,
            "w1": w1, "b1": b1, "w2": w2, "b2": b2, "ln2_g": ln2_g, "ln2_b": ln2_b}


def reference(x, c, ctx, c_ctx, ln0_g, ln0_b, w_ada, b_ada, w_in, mu_shift, w0, w2_decay, a0, a2_iclr,
              g2_gate, r_k, k_k, k_a, gn_g, gn_b, w_fno, b_fno, w_out, ln1_g, ln1_b, w_router, b_router,
              w1, b1, w2, b2, ln2_g, ln2_b):
    x = layer_norm(x, ln0_g, ln0_b)
    ctx = layer_norm(ctx, ln0_g, ln0_b)
    for l in range(DEPTH):
        need_ctx = l < DEPTH - 1
        mod = (jax.nn.silu(c) @ w_ada[l] + b_ada[l])[:, None, :]
        sh1, sc1, gt1, sh2, sc2, gt2 = jnp.split(mod, 6, axis=-1)
        mod_c = jax.nn.silu(c_ctx) @ w_ada[l] + b_ada[l]
        sh1c, sc1c, gt1c, sh2c, sc2c, gt2c = jnp.split(mod_c, 6, axis=-1)
        h = x * (1.0 + sc1) + sh1
        hc = ctx * (1.0 + sc1c) + sh1c
        mo, mo_c = token_mixer(h, hc, w_in[l], mu_shift[l], w0[l], w2_decay[l], a0[l], a2_iclr[l], g2_gate[l],
                               r_k[l], k_k[l], k_a[l], gn_g[l], gn_b[l], w_fno[l], b_fno[l], w_out[l], need_ctx)
        x = layer_norm(DEEPNORM_ALPHA * x + gt1 * mo, ln1_g[l], ln1_b[l])
        h = x * (1.0 + sc2) + sh2
        f = moe_ffn(h.reshape(-1, D_MODEL), w_router[l], b_router[l], w1[l], b1[l], w2[l], b2[l]).reshape(x.shape)
        x = layer_norm(DEEPNORM_ALPHA * x + gt2 * f, ln2_g[l], ln2_b[l])
        if need_ctx:
            ctx = layer_norm(DEEPNORM_ALPHA * ctx + gt1c * mo_c, ln1_g[l], ln1_b[l])
            hc = ctx * (1.0 + sc2c) + sh2c
            fc = moe_ffn(hc.reshape(-1, D_MODEL), w_router[l], b_router[l], w1[l], b1[l], w2[l], b2[l]).reshape(ctx.shape)
            ctx = layer_norm(DEEPNORM_ALPHA * ctx + gt2c * fc, ln2_g[l], ln2_b[l])
    return x
```

```python
import functools
import math

import numpy as np
import jax
import jax.numpy as jnp
from jax import lax
from jax.experimental import pallas as pl
from jax.experimental.pallas import tpu as pltpu

F32 = jnp.float32
BF16 = jnp.bfloat16
HIGHEST = lax.Precision.HIGHEST

GRID_W = 64
HEAD = 64
CHUNK = 64
TB = 256
N_EXPERTS = 32
TOP_K = 4
EBLK = 256
SWIGLU_LIMIT = 7.0
SWIGLU_ALPHA = 1.702
LN_EPS = 1e-5
GN_EPS = 64e-5
DEEPNORM_ALPHA = 2.0 ** 0.25
DECAY_SCALE = math.exp(-0.5)
VMEM_LIMIT = 56 * 1024 * 1024


def _cp(sem, vmem=VMEM_LIMIT):
    return pltpu.CompilerParams(dimension_semantics=sem, vmem_limit_bytes=vmem)


def _sigmoid(x):
    return 1.0 / (1.0 + jnp.exp(-x))


def _ln(u, g, b):
    mean = jnp.mean(u, -1, keepdims=True)
    d = u - mean
    var = jnp.mean(d * d, -1, keepdims=True)
    return d * lax.rsqrt(var + LN_EPS) * g + b


def _dot(a, b):
    return jnp.dot(a, b, preferred_element_type=F32)


def _dot_nt(a, b, precision=None):
    return lax.dot_general(a, b, (((1,), (1,)), ((), ())), precision=precision, preferred_element_type=F32)


def _split_dot(x, w):
    hi = x.astype(BF16)
    lo = (x - hi.astype(F32)).astype(BF16)
    return _dot(hi, w) + _dot(lo, w)


def _ada_kernel(c_ref, w_ref, b_ref, o_ref):
    c = c_ref[...]
    s = c * _sigmoid(c)
    o_ref[...] = jnp.dot(s, w_ref[...], precision=HIGHEST, preferred_element_type=F32) + b_ref[...]


def _ada(cc, w_ada, b_ada):
    rows, d = cc.shape
    n = w_ada.shape[1]
    tn = 1024
    return pl.pallas_call(
        _ada_kernel,
        grid=(n // tn,),
        in_specs=[pl.BlockSpec((rows, d), lambda j: (0, 0)),
                  pl.BlockSpec((d, tn), lambda j: (0, j)),
                  pl.BlockSpec((1, tn), lambda j: (0, j))],
        out_specs=pl.BlockSpec((rows, tn), lambda j: (0, j)),
        out_shape=jax.ShapeDtypeStruct((rows, n), F32),
        compiler_params=_cp(("parallel",)),
        name="ada",
    )(cc, w_ada, b_ada.reshape(1, n))


def _fold_kernel(cs_ref, wf_ref, win_ref, oc_ref, os_ref):
    wf = wf_ref[...]
    mc = jnp.dot(cs_ref[0], wf, precision=HIGHEST, preferred_element_type=F32)
    ms = jnp.dot(cs_ref[1], wf, precision=HIGHEST, preferred_element_type=F32)
    w = win_ref[...]
    oc_ref[...] = jnp.dot(w, mc, precision=HIGHEST, preferred_element_type=F32)
    os_ref[...] = jnp.dot(w, ms, precision=HIGHEST, preferred_element_type=F32)


def _fold(csc, w_fno, win_f3):
    g, d, gw = win_f3.shape
    return pl.pallas_call(
        _fold_kernel,
        grid=(g,),
        in_specs=[pl.BlockSpec((2, gw, gw), lambda i: (0, 0, 0)),
                  pl.BlockSpec((None, gw, gw), lambda i: (i, 0, 0)),
                  pl.BlockSpec((None, d, gw), lambda i: (i, 0, 0))],
        out_specs=[pl.BlockSpec((None, d, gw), lambda i: (i, 0, 0)),
                   pl.BlockSpec((None, d, gw), lambda i: (i, 0, 0))],
        out_shape=[jax.ShapeDtypeStruct((g, d, gw), F32)] * 2,
        compiler_params=_cp(("parallel",)),
        name="fold",
    )(csc, w_fno, win_f3)


def _in_kernel(x_ref, sc_ref, sh_ref, g_ref, b_ref, w_ref, s_ref, f_ref, *, shift_w, fw):
    xn = _ln(x_ref[...], g_ref[...], b_ref[...])
    h = xn * (1.0 + sc_ref[...]) + sh_ref[...]
    p = _dot(h.astype(BF16), w_ref[...])
    s_ref[...] = p[:, :shift_w]
    f_ref[0] = p[:, shift_w:shift_w + fw].astype(BF16)
    f_ref[1] = p[:, shift_w + fw:].astype(BF16)


def _inproj(xa, scsel, shsel, ln_g, ln_b, w3, seq, shift_w, fw):
    b, tall, d = xa.shape
    nt = tall // TB
    wn = w3.shape[1]
    return pl.pallas_call(
        functools.partial(_in_kernel, shift_w=shift_w, fw=fw),
        grid=(b, nt),
        in_specs=[pl.BlockSpec((None, TB, d), lambda i, t: (i, t, 0)),
                  pl.BlockSpec((None, None, 1, d), lambda i, t: (i, jnp.minimum(t, 1), 0, 0)),
                  pl.BlockSpec((None, None, 1, d), lambda i, t: (i, jnp.minimum(t, 1), 0, 0)),
                  pl.BlockSpec((1, d), lambda i, t: (0, 0)),
                  pl.BlockSpec((1, d), lambda i, t: (0, 0)),
                  pl.BlockSpec((d, wn), lambda i, t: (0, 0))],
        out_specs=[pl.BlockSpec((None, TB, shift_w), lambda i, t: (i, t, 0)),
                   pl.BlockSpec((2, TB, fw), lambda i, t: (0, jnp.maximum(t - 1, 0), i))],
        out_shape=[jax.ShapeDtypeStruct((b, tall, shift_w), F32),
                   jax.ShapeDtypeStruct((2, seq, b * fw), BF16)],
        compiler_params=_cp(("parallel", "arbitrary")),
        name="inproj",
    )(xa, scsel, shsel, ln_g, ln_b, w3)


def _prep_kernel(s_ref, sp_ref, sn_ref, mu_ref, w0_ref, w2d_ref, a0_ref, a2_ref, g2_ref, kkw_ref, ka_ref, ones_ref,
                 r_o, v_o, kk_o, lw_o, kd_o, bb_o, g_o, *, c):
    t = pl.program_id(1)
    nt = pl.num_programs(1)
    s = s_ref[...]
    idx = lax.broadcasted_iota(jnp.int32, (TB, 1), 0)
    col = idx & (GRID_W - 1)
    is_ctx = t == 0
    lmask = jnp.where(is_ctx, idx, col) == 0
    rmask = jnp.where(is_ctx, idx - (TB - 1), col - (GRID_W - 1)) == 0
    left = jnp.where(lmask, 0.0, pltpu.roll(s, 1, 0))
    right = jnp.where(rmask, 0.0, pltpu.roll(s, TB - 1, 0))
    up = jnp.concatenate([jnp.where(t == 1, 0.0, sp_ref[...]), s[:TB - GRID_W]], 0)
    down = jnp.concatenate([s[GRID_W:], jnp.where(t == nt - 1, 0.0, sn_ref[...])], 0)
    ud = jnp.where(is_ctx, 0.0, up + down)
    sh = (ud + left + right) * jnp.where(is_ctx, 0.5, 0.25)
    m = s + mu_ref[...] * (sh - s)

    r = m[:, :c]
    k = m[:, c:2 * c]
    v = m[:, 2 * c:3 * c]
    o = 3 * c
    nd = w2d_ref.shape[1]
    na = a2_ref.shape[1]
    ng = g2_ref.shape[1]
    wd = jnp.tanh(m[:, o:o + nd]).astype(BF16)
    ad = m[:, o + nd:o + nd + na].astype(BF16)
    gd = _sigmoid(m[:, o + nd + na:o + nd + na + ng]).astype(BF16)
    kk = k * kkw_ref[...]
    ss = _split_dot(kk * kk, ones_ref[...])
    kk = kk / jnp.maximum(jnp.sqrt(ss), 1e-12)
    r_o[...] = r
    v_o[...] = v
    kk_o[...] = kk
    for d in range(2):
        wl = w0_ref[d] + _dot(wd, w2d_ref[d])
        lw_o[d] = -DECAY_SCALE * _sigmoid(wl)
        a = _sigmoid(a0_ref[d] + _dot(ad, a2_ref[d]))
        g_o[d] = _dot(gd, g2_ref[d])
        kd_o[d] = k * (1.0 + (a - 1.0) * ka_ref[...])
        bb_o[d] = kk * a


def _prep(s_all, mu, w0, w2d, a0, a2, g2, k_k, k_a, ones_bd, c):
    b, tall, sw = s_all.shape
    nt = tall // TB
    nhb = tall // GRID_W
    hb = TB // GRID_W
    const2 = lambda i, t: (0, 0)
    const3 = lambda i, t: (0, 0, 0)
    o1 = pl.BlockSpec((None, TB, c), lambda i, t: (i, t, 0))
    o2 = pl.BlockSpec((2, None, TB, c), lambda i, t: (0, i, t, 0))
    s1 = jax.ShapeDtypeStruct((b, tall, c), F32)
    s2 = jax.ShapeDtypeStruct((2, b, tall, c), F32)
    return pl.pallas_call(
        functools.partial(_prep_kernel, c=c),
        grid=(b, nt),
        in_specs=[pl.BlockSpec((None, TB, sw), lambda i, t: (i, t, 0)),
                  pl.BlockSpec((None, GRID_W, sw), lambda i, t: (i, jnp.maximum(t * hb - 1, 0), 0)),
                  pl.BlockSpec((None, GRID_W, sw), lambda i, t: (i, jnp.minimum(t * hb + hb, nhb - 1), 0)),
                  pl.BlockSpec((1, sw), const2),
                  pl.BlockSpec(w0.shape, const3),
                  pl.BlockSpec(w2d.shape, const3),
                  pl.BlockSpec(a0.shape, const3),
                  pl.BlockSpec(a2.shape, const3),
                  pl.BlockSpec(g2.shape, const3),
                  pl.BlockSpec((1, c), const2),
                  pl.BlockSpec((1, c), const2),
                  pl.BlockSpec((c, c), const2)],
        out_specs=[o1, o1, o1, o2, o2, o2, o2],
        out_shape=[s1, s1, s1, s2, s2, s2, s2],
        compiler_params=_cp(("parallel", "arbitrary")),
        name="prep",
    )(s_all, s_all, s_all, mu, w0, w2d, a0, a2, g2, k_k, k_a, ones_bd)


def _bmm(a, b):
    return jnp.einsum("bij,bjk->bik", a.astype(BF16), b.astype(BF16), preferred_element_type=F32)


def _unit_lower_inverse_minus_eye(n):
    ti = lax.broadcasted_iota(jnp.int32, (1, CHUNK, CHUNK), 1)
    tj = lax.broadcasted_iota(jnp.int32, (1, CHUNK, CHUNK), 2)

    def same_block(shift):
        return (ti >> shift) == (tj >> shift)

    n8 = jnp.where(same_block(3), n, 0.0)
    n8s = _bmm(n8, n8)
    n8q = _bmm(n8s, n8s)
    e = n8 + n8s + _bmm(n8, n8s)
    e = e + n8q + _bmm(e, n8q)
    for shift in (3, 4, 5):
        off = jnp.where(same_block(shift + 1), jnp.where(same_block(shift), 0.0, n), 0.0)
        f = off + _bmm(e, off)
        e = e + f + _bmm(f, e)
    return e


def _scan_kernel(r_ref, v_ref, kk_ref, lw_ref, kd_ref, bb_ref, g_ref, rk_ref, gng_ref, gnb_ref, ts_ref, ones_ref,
                 o_ref, st_ref, y_ref, *, reverse):
    t = pl.program_id(1)
    width = r_ref.shape[-1]
    npair = width // (2 * HEAD)
    nchunk = TB // CHUNK

    @pl.when(t == 0)
    def _():
        st_ref[...] = jnp.zeros_like(st_ref)

    r = r_ref[...]
    v = v_ref[...]
    kk = kk_ref[...]
    lw = lw_ref[...]
    kd = kd_ref[...]
    bb = bb_ref[...]

    p1 = lw.astype(BF16)
    r1 = lw - p1.astype(F32)
    p2 = r1.astype(BF16)
    p3 = (r1 - p2.astype(F32)).astype(BF16)
    ts = ts_ref[...]
    acc = _dot(ts, p1) + _dot(ts, p2) + _dot(ts, p3)
    cl = acc[:TB]
    tot = acc[TB:]
    dec_in = jnp.exp(cl)
    dec_inv = jnp.exp(-cl)
    dec_ex = jnp.exp(cl - lw)
    dec_end = jnp.exp(tot - cl)
    dec_all = jnp.exp(tot)
    at = -(kk * dec_ex)
    bt = bb * dec_inv
    kt = kd * dec_inv
    rt = r * dec_in
    bh = bb * dec_end
    kh = kd * dec_end

    ti = lax.broadcasted_iota(jnp.int32, (CHUNK, 2 * HEAD), 0)
    lane = lax.broadcasted_iota(jnp.int32, (CHUNK, 2 * HEAD), 1)
    sj = lane & (HEAD - 1)
    if reverse:
        strict = sj > ti
        incl = sj >= ti
    else:
        strict = sj < ti
        incl = sj <= ti
    head0 = lane < HEAD
    bi = lax.broadcasted_iota(jnp.int32, (2 * HEAD, 2 * HEAD), 0)
    bj = lax.broadcasted_iota(jnp.int32, (2 * HEAD, 2 * HEAD), 1)
    same_head = (bi < HEAD) == (bj < HEAD)

    pre = {}
    nmats = []
    for c in range(nchunk):
        rs = slice(c * CHUNK, (c + 1) * CHUNK)
        for p in range(npair):
            ls = slice(p * 2 * HEAD, (p + 1) * 2 * HEAD)
            at_p = at[rs, ls]
            rt_p = rt[rs, ls]
            a0 = jnp.where(head0, at_p, 0.0)
            a1 = jnp.where(head0, 0.0, at_p)
            r0 = jnp.where(head0, rt_p, 0.0)
            r1_ = jnp.where(head0, 0.0, rt_p)
            lhs = jnp.concatenate([a0, a1, r0, r1_], 0).astype(BF16)
            bk = jnp.concatenate([bt[rs, ls], kt[rs, ls]], 0).astype(BF16)
            q = _dot_nt(lhs, bk)
            lk = [jnp.where(strict, q[h * CHUNK:(h + 1) * CHUNK], 0.0) for h in range(2)]
            mm = [jnp.where(incl, q[(2 + h) * CHUNK:(3 + h) * CHUNK], 0.0) for h in range(2)]
            nmats.append(lk[0][:, :CHUNK])
            nmats.append(lk[1][:, :CHUNK])
            pre[(c, p)] = (at_p, rt_p, (a0, a1), lk, mm, v[rs, ls])
    e_all = _unit_lower_inverse_minus_eye(jnp.stack(nmats, 0))

    fin = {}
    for c in range(nchunk):
        for p in range(npair):
            at_p, rt_p, a_h, lk, mm, v_p = pre[(c, p)]
            vv = jnp.concatenate([v_p, v_p], 0).astype(BF16)
            w_p = at_p
            u0 = None
            for h in range(2):
                hm = head0 if h == 0 else jnp.logical_not(head0)
                lkv = jnp.where(head0, 0.0, lk[h]).astype(BF16)
                lv = jnp.where(hm, _dot(lkv, vv), 0.0)
                rh = jnp.concatenate([a_h[h], lv], 1).astype(BF16)
                ew = _dot(e_all[(c * npair + p) * 2 + h].astype(BF16), rh)
                w_p = w_p + ew[:, :2 * HEAD]
                u0 = lv + ew[:, 2 * HEAD:] if u0 is None else u0 + lv + ew[:, 2 * HEAD:]
            fin[(c, p)] = (w_p, u0, rt_p, mm, v_p)

    order = range(nchunk - 1, -1, -1) if reverse else range(nchunk)
    for p in range(npair):
        ls = slice(p * 2 * HEAD, (p + 1) * 2 * HEAD)
        st = st_ref[p]
        for c in order:
            rs = slice(c * CHUNK, (c + 1) * CHUNK)
            w_p, u0, rt_p, mm, v_p = fin[(c, p)]
            wr = jnp.concatenate([w_p, rt_p], 0).astype(BF16)
            ws = _dot_nt(wr, st.astype(BF16))
            u = ws[:CHUNK] + u0
            uv = jnp.concatenate([u, v_p], 0)
            uvb = uv.astype(BF16)
            y = ws[CHUNK:] + jnp.where(head0, _dot(mm[0].astype(BF16), uvb), _dot(mm[1].astype(BF16), uvb))
            bkh = jnp.concatenate([bh[rs, ls], kh[rs, ls]], 0).astype(BF16)
            upd = _dot(uv.T.astype(BF16), bkh)
            st = st * dec_all[c * CHUNK:c * CHUNK + 1, ls] + jnp.where(same_head, upd, 0.0)
            y_ref[rs, ls] = y
        st_ref[p] = st

    ones = ones_ref[...]
    inv_n = 1.0 / HEAD
    y = y_ref[...]
    mean = _split_dot(y, ones) * inv_n
    d = y - mean
    var = _split_dot(d * d, ones) * inv_n
    yn = d * lax.rsqrt(var + GN_EPS) * gng_ref[...] + gnb_ref[...]
    bonus = _split_dot(r * kd * rk_ref[...], ones) * v
    o_ref[...] = (yn + bonus) * g_ref[...]


def _scan(r, v, kk, lw, kd, bb, g, rk, gn_g, gn_b, ts, ones_bd, direction):
    b, tall, c = r.shape
    nt = tall // TB
    reverse = direction == 1
    if reverse:
        tmap = lambda t: jnp.where(t == 0, 0, nt - t)
    else:
        tmap = lambda t: t
    shared = pl.BlockSpec((None, TB, c), lambda i, t: (i, tmap(t), 0))
    perdir = pl.BlockSpec((None, None, TB, c), lambda i, t: (direction, i, tmap(t), 0))
    const2 = lambda i, t: (0, 0)
    return pl.pallas_call(
        functools.partial(_scan_kernel, reverse=reverse),
        grid=(b, nt),
        in_specs=[shared, shared, shared, perdir, perdir, perdir, perdir,
                  pl.BlockSpec((None, 1, c), lambda i, t: (direction, 0, 0)),
                  pl.BlockSpec((1, c), const2),
                  pl.BlockSpec((1, c), const2),
                  pl.BlockSpec(ts.shape, const2),
                  pl.BlockSpec((c, c), const2)],
        out_specs=pl.BlockSpec((None, TB, c), lambda i, t: (i, tmap(t), 0)),
        out_shape=jax.ShapeDtypeStruct((b, tall, c), F32),
        scratch_shapes=[pltpu.VMEM((c // (2 * HEAD), 2 * HEAD, 2 * HEAD), F32),
                        pltpu.VMEM((TB, c), F32)],
        compiler_params=_cp(("parallel", "arbitrary")),
        name="scan_rev" if reverse else "scan_fwd",
    )(r, v, kk, lw, kd, bb, g, rk, gn_g, gn_b, ts, ones_bd)


def _mm_kernel(a_ref, b_ref, o_ref, acc_ref):
    k = pl.program_id(2)

    @pl.when(k == 0)
    def _():
        acc_ref[...] = jnp.zeros_like(acc_ref)

    acc_ref[...] += _dot(a_ref[...], b_ref[...])

    @pl.when(k == pl.num_programs(2) - 1)
    def _():
        o_ref[...] = acc_ref[...]


def _matmul(a, b, tm, tn, tk):
    m, kd = a.shape
    n = b.shape[1]
    tm, tn, tk = min(tm, m), min(tn, n), min(tk, kd)
    return pl.pallas_call(
        _mm_kernel,
        grid=(m // tm, n // tn, kd // tk),
        in_specs=[pl.BlockSpec((tm, tk), lambda i, j, k: (i, k)),
                  pl.BlockSpec((tk, tn), lambda i, j, k: (k, j))],
        out_specs=pl.BlockSpec((tm, tn), lambda i, j, k: (i, j)),
        out_shape=jax.ShapeDtypeStruct((m, n), F32),
        scratch_shapes=[pltpu.VMEM((tm, tn), F32)],
        compiler_params=_cp(("parallel", "parallel", "arbitrary")),
        name="dft",
    )(a, b)


def _dft_matrix(seq):
    hi = seq // 128
    k = np.arange(seq, dtype=np.int64)[:, None]
    a_ang = 2.0 * np.pi * ((k * np.arange(hi)[None, :] * 128) % seq) / seq
    b_ang = 2.0 * np.pi * ((k * np.arange(128)[None, :]) % seq) / seq
    ca = jnp.asarray(np.cos(a_ang), F32)[:, :, None]
    sa = jnp.asarray(np.sin(a_ang), F32)[:, :, None]
    cb = jnp.asarray(np.cos(b_ang), F32)[:, None, :]
    sb = jnp.asarray(np.sin(b_ang), F32)[:, None, :]
    cos = (ca * cb - sa * sb).reshape(seq, seq)
    sin = (sa * cb + ca * sb).reshape(seq, seq)
    return jnp.concatenate([cos, sin], 1).astype(BF16)


def _out_kernel(x_ref, o0_ref, o1_ref, fn_ref, g0_ref, b0_ref, gt_ref, sc_ref, sh_ref, wt_ref, wb_ref, bf_ref,
                g1_ref, b1_ref, wr_ref, br_ref, tri_ref, one_ref,
                x1_o, h2_o, idx_o, gate_o, rank_o, cnt_o):
    xn = _ln(x_ref[...], g0_ref[...], b0_ref[...])
    rw = (o0_ref[...] + o1_ref[...]).astype(BF16)
    fn = (fn_ref[...] + bf_ref[...]).astype(BF16)
    mo = _dot(rw, wt_ref[...]) + _dot(fn, wb_ref[...])
    x1 = _ln(DEEPNORM_ALPHA * xn + gt_ref[...] * mo, g1_ref[...], b1_ref[...])
    h2 = x1 * (1.0 + sc_ref[...]) + sh_ref[...]
    x1_o[...] = x1
    h2_o[...] = h2

    vals = _dot_nt(wr_ref[...], h2, precision=HIGHEST) + br_ref[...]
    ne = vals.shape[0]
    rowid = lax.broadcasted_iota(jnp.int32, vals.shape, 0)
    sels, tops, idxs = [], [], []
    for _ in range(TOP_K):
        mx = jnp.max(vals, axis=0, keepdims=True)
        ix = jnp.min(jnp.where(vals == mx, rowid, ne), axis=0, keepdims=True)
        sel = rowid == ix
        vals = jnp.where(sel, -jnp.inf, vals)
        sels.append(sel)
        tops.append(mx)
        idxs.append(ix)
    ex = [jnp.exp(m - tops[0]) for m in tops]
    den = ex[0] + ex[1] + ex[2] + ex[3]
    gate_o[...] = jnp.concatenate([e / den for e in ex], 0)
    idx_o[...] = jnp.concatenate(idxs, 0)
    onehot = jnp.zeros(vals.shape, F32)
    for sel in sels:
        onehot = onehot + jnp.where(sel, 1.0, 0.0)
    ohb = onehot.astype(BF16)
    before = _dot(ohb, tri_ref[...])
    rank_o[...] = jnp.concatenate(
        [jnp.sum(jnp.where(sel, before, 0.0), axis=0, keepdims=True) for sel in sels], 0).astype(jnp.int32)
    cnt_o[...] = _dot(ohb, one_ref[...])


def _outproj(x, o0, o1, fno, ln0_g, ln0_b, gt1, sc2, sh2, w_top, w_bot, b_fno, ln1_g, ln1_b, w_rt, b_r, tri, ones_col):
    b, seq, d = x.shape
    c = o0.shape[-1]
    nt = seq // TB
    ntile = b * nt
    ne = w_rt.shape[0]
    const2 = lambda i, t: (0, 0)
    vec = pl.BlockSpec((1, d), const2)
    mod = pl.BlockSpec((None, 1, d), lambda i, t: (i, 0, 0))
    tok = lambda i, t: (0, i * nt + t)
    return pl.pallas_call(
        _out_kernel,
        grid=(b, nt),
        in_specs=[pl.BlockSpec((None, TB, d), lambda i, t: (i, t, 0)),
                  pl.BlockSpec((None, TB, c), lambda i, t: (i, t + 1, 0)),
                  pl.BlockSpec((None, TB, c), lambda i, t: (i, t + 1, 0)),
                  pl.BlockSpec((TB, c), lambda i, t: (t, i)),
                  vec, vec, mod, mod, mod,
                  pl.BlockSpec((c, d), const2),
                  pl.BlockSpec((c, d), const2),
                  pl.BlockSpec((1, c), const2),
                  vec, vec,
                  pl.BlockSpec((ne, d), const2),
                  pl.BlockSpec((ne, 1), const2),
                  pl.BlockSpec((TB, TB), const2),
                  pl.BlockSpec((TB, 128), const2)],
        out_specs=[pl.BlockSpec((None, TB, d), lambda i, t: (i, t, 0)),
                   pl.BlockSpec((TB, d), lambda i, t: (i * nt + t, 0)),
                   pl.BlockSpec((TOP_K, TB), tok),
                   pl.BlockSpec((TOP_K, TB), tok),
                   pl.BlockSpec((TOP_K, TB), tok),
                   pl.BlockSpec((None, ne, 128), lambda i, t: (i * nt + t, 0, 0))],
        out_shape=[jax.ShapeDtypeStruct((b, seq, d), F32),
                   jax.ShapeDtypeStruct((b * seq, d), F32),
                   jax.ShapeDtypeStruct((TOP_K, b * seq), jnp.int32),
                   jax.ShapeDtypeStruct((TOP_K, b * seq), F32),
                   jax.ShapeDtypeStruct((TOP_K, b * seq), jnp.int32),
                   jax.ShapeDtypeStruct((ntile, ne, 128), F32)],
        compiler_params=_cp(("parallel", "arbitrary")),
        name="outproj",
    )(x, o0, o1, fno, ln0_g, ln0_b, gt1, sc2, sh2, w_top, w_bot, b_fno, ln1_g, ln1_b, w_rt, b_r, tri, ones_col)


def _slot_kernel(idx_ref, rank_ref, base_ref, o_ref):
    idx = idx_ref[...]
    base = base_ref[...]
    rowid = lax.broadcasted_iota(jnp.int32, (base.shape[0], idx.shape[1]), 0)
    rows = [jnp.sum(jnp.where(rowid == idx[k:k + 1], base, 0), axis=0, keepdims=True) for k in range(TOP_K)]
    o_ref[...] = rank_ref[...] + jnp.concatenate(rows, 0)


def _slots(idx, rank, base):
    ntile, ne, _ = base.shape
    return pl.pallas_call(
        _slot_kernel,
        grid=(ntile,),
        in_specs=[pl.BlockSpec((TOP_K, TB), lambda i: (0, i)),
                  pl.BlockSpec((TOP_K, TB), lambda i: (0, i)),
                  pl.BlockSpec((None, ne, 1), lambda i: (i, 0, 0))],
        out_specs=pl.BlockSpec((None, TOP_K, TB), lambda i: (i, 0, 0)),
        out_shape=jax.ShapeDtypeStruct((ntile, TOP_K, TB), jnp.int32),
        compiler_params=_cp(("parallel",)),
        name="slot",
    )(idx, rank, base)


def _row_copy(src, src_row, dst, dst_row, sem):
    return pltpu.make_async_copy(src.at[pl.ds(src_row, 1)], dst.at[pl.ds(dst_row, 1)], sem)


def _dispatch_kernel(pad_ref, slot_hbm, h_ref, xs_ref, slot_smem, zrow_ref, sem, ssem):
    i = pl.program_id(0)
    cp = pltpu.make_async_copy(slot_hbm.at[i], slot_smem, ssem)
    cp.start()
    cp.wait()

    def issue(j, carry):
        for k in range(TOP_K):
            _row_copy(h_ref, j, xs_ref, slot_smem[k, j], sem).start()
        return carry

    lax.fori_loop(0, TB, issue, 0)

    def drain(j, carry):
        for k in range(TOP_K):
            _row_copy(h_ref, 0, xs_ref, 0, sem).wait()
        return carry

    lax.fori_loop(0, TB, drain, 0)

    @pl.when(i == pl.num_programs(0) - 1)
    def _():
        zrow_ref[...] = jnp.zeros_like(zrow_ref)

        def per_expert(e, carry):
            start = pad_ref[e]
            n = pad_ref[N_EXPERTS + e]

            def zi(q, c2):
                _row_copy(zrow_ref, 0, xs_ref, start + q, sem).start()
                return c2

            lax.fori_loop(0, n, zi, 0)

            def zw(q, c2):
                _row_copy(zrow_ref, 0, xs_ref, 0, sem).wait()
                return c2

            lax.fori_loop(0, n, zw, 0)
            return carry

        lax.fori_loop(0, N_EXPERTS, per_expert, 0)


def _dispatch(padinfo, slot, h2, n_slots):
    n, d = h2.shape
    ntile = n // TB
    return pl.pallas_call(
        _dispatch_kernel,
        grid_spec=pltpu.PrefetchScalarGridSpec(
            num_scalar_prefetch=1,
            grid=(ntile,),
            in_specs=[pl.BlockSpec(memory_space=pl.ANY),
                      pl.BlockSpec((TB, d), lambda i, pad: (i, 0))],
            out_specs=pl.BlockSpec(memory_space=pl.ANY),
            scratch_shapes=[pltpu.SMEM((TOP_K, TB), jnp.int32),
                            pltpu.VMEM((1, d), F32),
                            pltpu.SemaphoreType.DMA,
                            pltpu.SemaphoreType.DMA]),
        out_shape=jax.ShapeDtypeStruct((n_slots, d), F32),
        compiler_params=_cp(("arbitrary",)),
        name="dispatch",
    )(padinfo, slot, h2)


def _expert_kernel(be_ref, nu_ref, x_ref, w1_ref, b1_ref, w2_ref, b2_ref, perm_ref, o_ref, w1b_ref, w2b_ref):
    i = pl.program_id(0)
    prev = be_ref[jnp.maximum(i - 1, 0)]
    changed = jnp.logical_or(i == 0, be_ref[i] != prev)
    dff2 = w1_ref.shape[1]
    nblk = dff2 // 256

    @pl.when(changed)
    def _():
        perm = perm_ref[...]
        for j in range(nblk):
            cs = slice(j * 256, (j + 1) * 256)
            w1b_ref[:, cs] = _dot(w1_ref[:, cs].astype(BF16), perm).astype(BF16)
        w2b_ref[...] = w2_ref[...].astype(BF16)

    @pl.when(i < nu_ref[0])
    def _():
        u = _dot(x_ref[...].astype(BF16), w1b_ref[...]) + b1_ref[...]
        acts = []
        for j in range(nblk):
            glu = jnp.minimum(u[:, j * 256:j * 256 + 128], SWIGLU_LIMIT)
            lin = jnp.clip(u[:, j * 256 + 128:(j + 1) * 256], -SWIGLU_LIMIT, SWIGLU_LIMIT)
            acts.append(glu * _sigmoid(SWIGLU_ALPHA * glu) * (lin + 1.0))
        act = jnp.concatenate(acts, 1).astype(BF16)
        o_ref[...] = _dot(act, w2b_ref[...]) + b2_ref[...]

    @pl.when(i >= nu_ref[0])
    def _():
        o_ref[...] = jnp.zeros_like(o_ref)


def _experts(block_e, n_used, xs, w1, b1p, w2, b2, perm):
    n_slots, d = xs.shape
    ne, _, dff2 = w1.shape
    dff = w2.shape[1]
    nblocks = n_slots // EBLK
    return pl.pallas_call(
        _expert_kernel,
        grid_spec=pltpu.PrefetchScalarGridSpec(
            num_scalar_prefetch=2,
            grid=(nblocks,),
            in_specs=[pl.BlockSpec((EBLK, d), lambda i, be, nu: (jnp.minimum(i, nu[0] - 1), 0)),
                      pl.BlockSpec((None, d, dff2), lambda i, be, nu: (be[i], 0, 0)),
                      pl.BlockSpec((None, 1, dff2), lambda i, be, nu: (be[i], 0, 0)),
                      pl.BlockSpec((None, dff, d), lambda i, be, nu: (be[i], 0, 0)),
                      pl.BlockSpec((None, 1, d), lambda i, be, nu: (be[i], 0, 0)),
                      pl.BlockSpec((256, 256), lambda i, be, nu: (0, 0))],
            out_specs=pl.BlockSpec((EBLK, d), lambda i, be, nu: (i, 0)),
            scratch_shapes=[pltpu.VMEM((d, dff2), BF16),
                            pltpu.VMEM((dff, d), BF16)]),
        out_shape=jax.ShapeDtypeStruct((n_slots, d), F32),
        compiler_params=_cp(("arbitrary",)),
        name="experts",
    )(block_e, n_used, xs, w1, b1p, w2, b2, perm)


def _combine_kernel(slot_hbm, ys_ref, gate_ref, x1_ref, gt_ref, g_ref, b_ref, eye_ref, o_ref, slot_smem, buf_ref, sem, ssem):
    i = pl.program_id(0)
    cp = pltpu.make_async_copy(slot_hbm.at[i], slot_smem, ssem)
    cp.start()
    cp.wait()

    def issue(j, carry):
        for k in range(TOP_K):
            _row_copy(ys_ref, slot_smem[k, j], buf_ref.at[k], j, sem).start()
        return carry

    lax.fori_loop(0, TB, issue, 0)

    def drain(j, carry):
        for k in range(TOP_K):
            _row_copy(ys_ref, 0, buf_ref.at[k], 0, sem).wait()
        return carry

    lax.fori_loop(0, TB, drain, 0)

    gt = _dot_nt(eye_ref[...], gate_ref[...], precision=HIGHEST)
    f = gt[:, 0:1] * buf_ref[0]
    for k in range(1, TOP_K):
        f = f + gt[:, k:k + 1] * buf_ref[k]
    o_ref[...] = _ln(DEEPNORM_ALPHA * x1_ref[...] + gt_ref[...] * f, g_ref[...], b_ref[...])


def _combine(slot, ys, gates, x1, gt2, ln_g, ln_b, eye, tiles_per_batch):
    n, d = x1.shape
    ntile = n // TB
    return pl.pallas_call(
        _combine_kernel,
        grid=(ntile,),
        in_specs=[pl.BlockSpec(memory_space=pl.ANY),
                  pl.BlockSpec(memory_space=pl.ANY),
                  pl.BlockSpec((TOP_K, TB), lambda i: (0, i)),
                  pl.BlockSpec((TB, d), lambda i: (i, 0)),
                  pl.BlockSpec((None, 1, d), lambda i: (i // tiles_per_batch, 0, 0)),
                  pl.BlockSpec((1, d), lambda i: (0, 0)),
                  pl.BlockSpec((1, d), lambda i: (0, 0)),
                  pl.BlockSpec((TB, TB), lambda i: (0, 0))],
        out_specs=pl.BlockSpec((TB, d), lambda i: (i, 0)),
        out_shape=jax.ShapeDtypeStruct((n, d), F32),
        scratch_shapes=[pltpu.SMEM((TOP_K, TB), jnp.int32),
                        pltpu.VMEM((TOP_K, TB, d), F32),
                        pltpu.SemaphoreType.DMA,
                        pltpu.SemaphoreType.DMA],
        compiler_params=_cp(("arbitrary",)),
        name="combine",
    )(slot, ys, gates, x1, gt2, ln_g, ln_b, eye)


def _scan_tables(reverse):
    t = np.arange(TB)
    same = (t[:, None] // CHUNK) == (t[None, :] // CHUNK)
    tri = (t[None, :] >= t[:, None]) if reverse else (t[None, :] <= t[:, None])
    return jnp.asarray(np.concatenate([same & tri, same], 0).astype(np.float32), BF16)


def _head_ones(c):
    h = np.arange(c) // HEAD
    return jnp.asarray((h[:, None] == h[None, :]).astype(np.float32), BF16)


def _deinterleave_perm():
    p = np.zeros((256, 256), np.float32)
    j = np.arange(128)
    p[2 * j, j] = 1.0
    p[2 * j + 1, 128 + j] = 1.0
    return jnp.asarray(p, BF16)


def kernel(x, c, ctx, c_ctx, ln0_g, ln0_b, w_ada, b_ada, w_in, mu_shift, w0, w2_decay, a0, a2_iclr, g2_gate, r_k, k_k, k_a, gn_g, gn_b, w_fno, b_fno, w_out, ln1_g, ln1_b, w_router, b_router, w1, b1, w2, b2, ln2_g, ln2_b):
    b, seq, d = x.shape
    ctx_len = ctx.shape[1]
    assert ctx_len == TB and seq % TB == 0 and w_ada.shape[0] == 1
    n_dir, cw = w0.shape[1], w0.shape[2]
    fgroups, gw = w_fno.shape[1], w_fno.shape[2]
    fw = fgroups * gw
    shift_w = mu_shift.shape[1]
    nd, na, ng = w2_decay.shape[2], a2_iclr.shape[2], g2_gate.shape[2]
    assert n_dir == 2 and shift_w == 3 * cw + 2 * (nd + na + ng)
    ne = w_router.shape[2]
    n_tok = b * seq
    row = lambda a: a.reshape(1, -1)

    rows = -(-(b + 1) // 8) * 8
    cc = jnp.zeros((rows, d), F32).at[:b].set(c).at[b].set(c_ctx)
    mod = _ada(cc, w_ada[0], b_ada[0])
    sh1, sc1, gt1, sh2, sc2, gt2 = [mod[:b, i * d:(i + 1) * d] for i in range(6)]
    sh1c, sc1c = mod[b, :d], mod[b, d:2 * d]
    scsel = jnp.stack([jnp.broadcast_to(sc1c, (b, d)), sc1], 1)[:, :, None, :]
    shsel = jnp.stack([jnp.broadcast_to(sh1c, (b, d)), sh1], 1)[:, :, None, :]
    mod3 = lambda a: a[:, None, :]

    cidx = np.arange(gw)
    ang = 2.0 * np.pi * ((cidx[:, None] * cidx[None, :]) % gw) / gw
    norm = 1.0 / math.sqrt(seq * gw)
    csc = jnp.asarray(np.stack([np.cos(ang) * norm, -np.sin(ang) * norm]), F32)
    win_f3 = w_in[0][:, shift_w:].reshape(d, fgroups, gw).transpose(1, 0, 2)
    wfc, wfs = _fold(csc, w_fno[0], win_f3)
    unf = lambda a: a.transpose(1, 0, 2).reshape(d, fw)
    w3 = jnp.concatenate([w_in[0][:, :shift_w], unf(wfc), unf(wfs)], 1).astype(BF16)

    xa = jnp.concatenate([ctx, x], 1)
    s_all, fcat = _inproj(xa, scsel, shsel, row(ln0_g), row(ln0_b), w3, seq, shift_w, fw)

    def pad_dir(w, width):
        out = jnp.zeros((2, 2 * width, cw), F32)
        return out.at[0, :width].set(w[0]).at[1, width:].set(w[1]).astype(BF16)

    ones_bd = _head_ones(cw)
    r, v, kk, lw, kd, bb, g = _prep(
        s_all, row(mu_shift[0]), w0[0][:, None, :], pad_dir(w2_decay[0], nd), a0[0][:, None, :],
        pad_dir(a2_iclr[0], na), pad_dir(g2_gate[0], ng), row(k_k[0]), row(k_a[0]), ones_bd, cw)

    rk = r_k[0].reshape(2, 1, cw)
    outs = [_scan(r, v, kk, lw, kd, bb, g, rk, row(gn_g[0]), row(gn_b[0]), _scan_tables(dr == 1), ones_bd, dr)
            for dr in range(2)]

    fno = _matmul(_dft_matrix(seq), fcat.reshape(2 * seq, b * fw), 1024, 2048, 1024)

    ids = np.arange(TB)
    tri = jnp.asarray((ids[:, None] < ids[None, :]).astype(np.float32), BF16)
    ones_col = jnp.ones((TB, 128), BF16)
    wo = w_out[0].astype(BF16)
    x1, h2, idx, gates, rank, cnt = _outproj(
        x, outs[0], outs[1], fno, row(ln0_g), row(ln0_b), mod3(gt1), mod3(sc2), mod3(sh2), wo[:cw], wo[cw:],
        row(b_fno[0]), row(ln1_g[0]), row(ln1_b[0]), w_router[0].T, b_router[0].reshape(ne, 1), tri, ones_col)

    ntile = n_tok // TB
    cnt_t = cnt[:, :, 0].astype(jnp.int32)
    counts = jnp.sum(cnt_t, 0)
    padded = (counts + EBLK - 1) // EBLK * EBLK
    pends = jnp.cumsum(padded)
    pstarts = pends - padded
    base = (pstarts[None, :] + jnp.cumsum(cnt_t, 0) - cnt_t)[:, :, None]
    nblocks = (n_tok * TOP_K) // EBLK + ne
    n_slots = nblocks * EBLK
    block_e = jnp.minimum(jnp.searchsorted(pends, jnp.arange(nblocks, dtype=jnp.int32) * EBLK, side="right"),
                          ne - 1).astype(jnp.int32)
    n_used = (pends[-1:] // EBLK).astype(jnp.int32)
    padinfo = jnp.concatenate([pstarts + counts, padded - counts]).astype(jnp.int32)

    slot = _slots(idx, rank, base)
    xs = _dispatch(padinfo, slot, h2, n_slots)
    b1p = b1[0].reshape(ne, -1, 128, 2).transpose(0, 1, 3, 2).reshape(ne, 1, -1)
    ys = _experts(block_e, n_used, xs, w1[0], b1p, w2[0], b2[0][:, None, :], _deinterleave_perm())
    out = _combine(slot, ys, gates, x1.reshape(n_tok, d), mod3(gt2), row(ln2_g[0]), row(ln2_b[0]),
                   jnp.eye(TB, dtype=F32), seq // TB)
    return out.reshape(b, seq, d)
```

```python
import functools
import math

import numpy as np
import jax
import jax.numpy as jnp
from jax import lax
from jax.experimental import pallas as pl
from jax.experimental.pallas import tpu as pltpu

F32 = jnp.float32
BF16 = jnp.bfloat16
HIGHEST = lax.Precision.HIGHEST

GRID_W = 64
HEAD = 64
CHUNK = 64
TB = 256
N_EXPERTS = 32
TOP_K = 4
EBLK = 256
SWIGLU_LIMIT = 7.0
SWIGLU_ALPHA = 1.702
LN_EPS = 1e-5
GN_EPS = 64e-5
DEEPNORM_ALPHA = 2.0 ** 0.25
DECAY_SCALE = math.exp(-0.5)
VMEM_LIMIT = 56 * 1024 * 1024


def _cp(sem, vmem=VMEM_LIMIT):
    return pltpu.CompilerParams(dimension_semantics=sem, vmem_limit_bytes=vmem)


def _sigmoid(x):
    return 1.0 / (1.0 + jnp.exp(-x))


def _ln(u, g, b):
    mean = jnp.mean(u, -1, keepdims=True)
    d = u - mean
    var = jnp.mean(d * d, -1, keepdims=True)
    return d * lax.rsqrt(var + LN_EPS) * g + b


def _dot(a, b):
    return jnp.dot(a, b, preferred_element_type=F32)


def _dot_nt(a, b, precision=None):
    return lax.dot_general(a, b, (((1,), (1,)), ((), ())), precision=precision, preferred_element_type=F32)


def _split_dot(x, w):
    hi = x.astype(BF16)
    lo = (x - hi.astype(F32)).astype(BF16)
    return _dot(hi, w) + _dot(lo, w)


def _ada_kernel(c_ref, w_ref, b_ref, o_ref):
    c = c_ref[...]
    s = c * _sigmoid(c)
    o_ref[...] = jnp.dot(s, w_ref[...], precision=HIGHEST, preferred_element_type=F32) + b_ref[...]


def _ada(cc, w_ada, b_ada):
    rows, d = cc.shape
    n = w_ada.shape[1]
    tn = 1024
    return pl.pallas_call(
        _ada_kernel,
        grid=(n // tn,),
        in_specs=[pl.BlockSpec((rows, d), lambda j: (0, 0)),
                  pl.BlockSpec((d, tn), lambda j: (0, j)),
                  pl.BlockSpec((1, tn), lambda j: (0, j))],
        out_specs=pl.BlockSpec((rows, tn), lambda j: (0, j)),
        out_shape=jax.ShapeDtypeStruct((rows, n), F32),
        compiler_params=_cp(("parallel",)),
        name="ada",
    )(cc, w_ada, b_ada.reshape(1, n))


def _fold_kernel(cs_ref, wf_ref, win_ref, oc_ref, os_ref):
    wf = wf_ref[...]
    mc = jnp.dot(cs_ref[0], wf, precision=HIGHEST, preferred_element_type=F32)
    ms = jnp.dot(cs_ref[1], wf, precision=HIGHEST, preferred_element_type=F32)
    w = win_ref[...]
    oc_ref[...] = jnp.dot(w, mc, precision=HIGHEST, preferred_element_type=F32)
    os_ref[...] = jnp.dot(w, ms, precision=HIGHEST, preferred_element_type=F32)


def _fold(csc, w_fno, win_f3):
    g, d, gw = win_f3.shape
    return pl.pallas_call(
        _fold_kernel,
        grid=(g,),
        in_specs=[pl.BlockSpec((2, gw, gw), lambda i: (0, 0, 0)),
                  pl.BlockSpec((None, gw, gw), lambda i: (i, 0, 0)),
                  pl.BlockSpec((None, d, gw), lambda i: (i, 0, 0))],
        out_specs=[pl.BlockSpec((None, d, gw), lambda i: (i, 0, 0)),
                   pl.BlockSpec((None, d, gw), lambda i: (i, 0, 0))],
        out_shape=[jax.ShapeDtypeStruct((g, d, gw), F32)] * 2,
        compiler_params=_cp(("parallel",)),
        name="fold",
    )(csc, w_fno, win_f3)


def _in_kernel(ctx_ref, x_ref, sc_ref, sh_ref, g_ref, b_ref, w_ref, s_ref, f_ref, *, shift_w, fw):
    xin = jnp.where(pl.program_id(1) == 0, ctx_ref[...], x_ref[...])
    xn = _ln(xin, g_ref[...], b_ref[...])
    h = xn * (1.0 + sc_ref[...]) + sh_ref[...]
    p = _dot(h.astype(BF16), w_ref[...])
    s_ref[...] = p[:, :shift_w]
    f_ref[0] = p[:, shift_w:shift_w + fw].astype(BF16)
    f_ref[1] = p[:, shift_w + fw:].astype(BF16)


def _inproj(ctx, x, scsel, shsel, ln_g, ln_b, w3, shift_w, fw):
    b, seq, d = x.shape
    tall = ctx.shape[1] + seq
    nt = tall // TB
    wn = w3.shape[1]
    return pl.pallas_call(
        functools.partial(_in_kernel, shift_w=shift_w, fw=fw),
        grid=(b, nt),
        in_specs=[pl.BlockSpec((None, TB, d), lambda i, t: (i, 0, 0)),
                  pl.BlockSpec((None, TB, d), lambda i, t: (i, jnp.maximum(t - 1, 0), 0)),
                  pl.BlockSpec((None, None, 1, d), lambda i, t: (i, jnp.minimum(t, 1), 0, 0)),
                  pl.BlockSpec((None, None, 1, d), lambda i, t: (i, jnp.minimum(t, 1), 0, 0)),
                  pl.BlockSpec((1, d), lambda i, t: (0, 0)),
                  pl.BlockSpec((1, d), lambda i, t: (0, 0)),
                  pl.BlockSpec((d, wn), lambda i, t: (0, 0))],
        out_specs=[pl.BlockSpec((None, TB, shift_w), lambda i, t: (i, t, 0)),
                   pl.BlockSpec((2, TB, fw), lambda i, t: (0, jnp.maximum(t - 1, 0), i))],
        out_shape=[jax.ShapeDtypeStruct((b, tall, shift_w), F32),
                   jax.ShapeDtypeStruct((2, seq, b * fw), BF16)],
        compiler_params=_cp(("parallel", "arbitrary")),
        name="inproj",
    )(ctx, x, scsel, shsel, ln_g, ln_b, w3)


def _prep_kernel(s_ref, sp_ref, sn_ref, mu_ref, w0_ref, w2d_ref, a0_ref, a2_ref, g2_ref, kkw_ref, ka_ref, ones_ref,
                 r_o, v_o, kk_o, lw_o, kd_o, bb_o, g_o, *, c):
    t = pl.program_id(1)
    nt = pl.num_programs(1)
    s = s_ref[...]
    idx = lax.broadcasted_iota(jnp.int32, (TB, 1), 0)
    col = idx & (GRID_W - 1)
    is_ctx = t == 0
    lmask = jnp.where(is_ctx, idx, col) == 0
    rmask = jnp.where(is_ctx, idx - (TB - 1), col - (GRID_W - 1)) == 0
    left = jnp.where(lmask, 0.0, pltpu.roll(s, 1, 0))
    right = jnp.where(rmask, 0.0, pltpu.roll(s, TB - 1, 0))
    up = jnp.concatenate([jnp.where(t == 1, 0.0, sp_ref[...]), s[:TB - GRID_W]], 0)
    down = jnp.concatenate([s[GRID_W:], jnp.where(t == nt - 1, 0.0, sn_ref[...])], 0)
    ud = jnp.where(is_ctx, 0.0, up + down)
    sh = (ud + left + right) * jnp.where(is_ctx, 0.5, 0.25)
    m = s + mu_ref[...] * (sh - s)

    r = m[:, :c]
    k = m[:, c:2 * c]
    v = m[:, 2 * c:3 * c]
    o = 3 * c
    nd = w2d_ref.shape[1]
    na = a2_ref.shape[1]
    ng = g2_ref.shape[1]
    wd = jnp.tanh(m[:, o:o + nd]).astype(BF16)
    ad = m[:, o + nd:o + nd + na].astype(BF16)
    gd = _sigmoid(m[:, o + nd + na:o + nd + na + ng]).astype(BF16)
    kk = k * kkw_ref[...]
    ss = _split_dot(kk * kk, ones_ref[...])
    kk = kk / jnp.maximum(jnp.sqrt(ss), 1e-12)
    r_o[...] = r
    v_o[...] = v
    kk_o[...] = kk
    for d in range(2):
        wl = w0_ref[d] + _dot(wd, w2d_ref[d])
        lw_o[d] = -DECAY_SCALE * _sigmoid(wl)
        a = _sigmoid(a0_ref[d] + _dot(ad, a2_ref[d]))
        g_o[d] = _dot(gd, g2_ref[d])
        kd_o[d] = k * (1.0 + (a - 1.0) * ka_ref[...])
        bb_o[d] = kk * a


def _prep(s_all, mu, w0, w2d, a0, a2, g2, k_k, k_a, ones_bd, c):
    b, tall, sw = s_all.shape
    nt = tall // TB
    nhb = tall // GRID_W
    hb = TB // GRID_W
    const2 = lambda i, t: (0, 0)
    const3 = lambda i, t: (0, 0, 0)
    o1 = pl.BlockSpec((None, TB, c), lambda i, t: (i, t, 0))
    o2 = pl.BlockSpec((2, None, TB, c), lambda i, t: (0, i, t, 0))
    s1 = jax.ShapeDtypeStruct((b, tall, c), F32)
    s2 = jax.ShapeDtypeStruct((2, b, tall, c), F32)
    return pl.pallas_call(
        functools.partial(_prep_kernel, c=c),
        grid=(b, nt),
        in_specs=[pl.BlockSpec((None, TB, sw), lambda i, t: (i, t, 0)),
                  pl.BlockSpec((None, GRID_W, sw), lambda i, t: (i, jnp.maximum(t * hb - 1, 0), 0)),
                  pl.BlockSpec((None, GRID_W, sw), lambda i, t: (i, jnp.minimum(t * hb + hb, nhb - 1), 0)),
                  pl.BlockSpec((1, sw), const2),
                  pl.BlockSpec(w0.shape, const3),
                  pl.BlockSpec(w2d.shape, const3),
                  pl.BlockSpec(a0.shape, const3),
                  pl.BlockSpec(a2.shape, const3),
                  pl.BlockSpec(g2.shape, const3),
                  pl.BlockSpec((1, c), const2),
                  pl.BlockSpec((1, c), const2),
                  pl.BlockSpec((c, c), const2)],
        out_specs=[o1, o1, o1, o2, o2, o2, o2],
        out_shape=[s1, s1, s1, s2, s2, s2, s2],
        compiler_params=_cp(("parallel", "arbitrary")),
        name="prep",
    )(s_all, s_all, s_all, mu, w0, w2d, a0, a2, g2, k_k, k_a, ones_bd)


QUAD = 4 * HEAD


def _unit_triangular_inverses(ns):
    ti = lax.broadcasted_iota(jnp.int32, ns[0].shape, 0)
    tj = lax.broadcasted_iota(jnp.int32, ns[0].shape, 1)

    def same_block(shift):
        return (ti >> shift) == (tj >> shift)

    zero = jnp.zeros_like(ns[0])
    n8 = [jnp.where(same_block(3), n, zero) for n in ns]
    n8s = [_dot(a, a).astype(BF16) for a in n8]
    n8q = [_dot(a, a).astype(BF16) for a in n8s]
    xs = [jnp.where(ti == tj, jnp.ones_like(a), a) for a in n8]
    xs = [(x.astype(F32) + _dot(x, a)).astype(BF16) for x, a in zip(xs, n8s)]
    xs = [(x.astype(F32) + _dot(x, a)).astype(BF16) for x, a in zip(xs, n8q)]
    for shift in (3, 4, 5):
        offs = [jnp.where(same_block(shift + 1), jnp.where(same_block(shift), zero, n), zero) for n in ns]
        fs = [_dot(x, off).astype(BF16) for x, off in zip(xs, offs)]
        xs = [(x.astype(F32) + _dot(f, x)).astype(BF16) for x, f in zip(xs, fs)]
    return xs


def _scan_kernel(r_ref, v_ref, kk_ref, lw_ref, kd_ref, bb_ref, g_ref, rk_ref, gng_ref, gnb_ref, ts_ref, ones_ref,
                 o_ref, st_ref, y_ref, *, reverse):
    t = pl.program_id(1)
    width = r_ref.shape[-1]
    nquad = width // QUAD
    nchunk = TB // CHUNK
    nh = QUAD // HEAD

    @pl.when(t == 0)
    def _():
        st_ref[...] = jnp.zeros_like(st_ref)

    r = r_ref[...]
    v = v_ref[...]
    kk = kk_ref[...]
    lw = lw_ref[...]
    kd = kd_ref[...]
    bb = bb_ref[...]

    p1 = lw.astype(BF16)
    r1 = lw - p1.astype(F32)
    p2 = r1.astype(BF16)
    p3 = (r1 - p2.astype(F32)).astype(BF16)
    ts = ts_ref[...]
    acc = _dot(ts, p1) + _dot(ts, p2) + _dot(ts, p3)
    cl = acc[:TB]
    tot = acc[TB:]
    dec_in = jnp.exp(cl)
    dec_inv = jnp.exp(-cl)
    dec_ex = jnp.exp(cl - lw)
    dec_end = jnp.exp(tot - cl)
    dec_all = jnp.exp(tot)
    at = (-(kk * dec_ex)).astype(BF16)
    bt = (bb * dec_inv).astype(BF16)
    kt = (kd * dec_inv).astype(BF16)
    rt = (r * dec_in).astype(BF16)
    bh = (bb * dec_end).astype(BF16)
    kh = (kd * dec_end).astype(BF16)
    vb = v.astype(BF16)

    lane_head = lax.broadcasted_iota(jnp.int32, (CHUNK, QUAD), 1) // HEAD
    qi = lax.broadcasted_iota(jnp.int32, (QUAD, QUAD), 0)
    qj = lax.broadcasted_iota(jnp.int32, (QUAD, QUAD), 1)
    ti = qi & (CHUNK - 1)
    sj = qj & (CHUNK - 1)
    strict = (sj > ti) if reverse else (sj < ti)
    incl = (sj >= ti) if reverse else (sj <= ti)
    same_head = (qi // HEAD) == (qj // HEAD)

    def stack_heads(x):
        return jnp.concatenate([jnp.where(lane_head == h, x, jnp.zeros_like(x)) for h in range(nh)], 0)

    def fold_heads(x):
        out = x[:CHUNK]
        for h in range(1, nh):
            out = out + x[h * CHUNK:(h + 1) * CHUNK]
        return out

    units = [(c, q) for c in range(nchunk) for q in range(nquad)]
    pre = {}
    for c, q in units:
        rs = slice(c * CHUNK, (c + 1) * CHUNK)
        ls = slice(q * QUAD, (q + 1) * QUAD)
        a_st = stack_heads(at[rs, ls])
        r_st = stack_heads(rt[rs, ls])
        b_st = stack_heads(bt[rs, ls])
        k_st = stack_heads(kt[rs, ls])
        v_st = stack_heads(vb[rs, ls])
        qq = _dot_nt(jnp.concatenate([a_st, r_st], 0), jnp.concatenate([b_st, k_st], 0))
        n_ab = jnp.where(strict, qq[:QUAD, :QUAD], 0.0).astype(BF16)
        l_ak = jnp.where(strict, qq[:QUAD, QUAD:], 0.0).astype(BF16)
        m_r = jnp.concatenate([jnp.where(incl, qq[QUAD:, :QUAD], 0.0),
                               jnp.where(incl, qq[QUAD:, QUAD:], 0.0)], 1).astype(BF16)
        pre[(c, q)] = (a_st, v_st, n_ab, l_ak, m_r)
    inv = _unit_triangular_inverses([pre[u][2] for u in units])
    zq = jnp.zeros((CHUNK, QUAD), BF16)
    zf = jnp.zeros((CHUNK, QUAD), F32)
    rows = [slice(c * CHUNK, (c + 1) * CHUNK) for c, q in units]
    lanes = [slice(q * QUAD, (q + 1) * QUAD) for c, q in units]
    a_sts, v_sts, _, l_aks, m_rs = zip(*[pre[u] for u in units])
    lvs = [_dot(l_ak, v_st).astype(BF16) for l_ak, v_st in zip(l_aks, v_sts)]
    xws = [_dot(x, jnp.concatenate([a_st, lv], 1)) for x, a_st, lv in zip(inv, a_sts, lvs)]
    w_sts = [xw[:, :QUAD] for xw in xws]
    u0_sts = [xw[:, QUAD:] for xw in xws]
    rhats = [rt[rs, ls].astype(F32) + fold_heads(_dot(m_r[:, :QUAD], w_st.astype(BF16)))
             for rs, ls, m_r, w_st in zip(rows, lanes, m_rs, w_sts)]
    ycs = [fold_heads(_dot(m_r, jnp.concatenate([u0_st.astype(BF16), v_st], 0)))
           for m_r, u0_st, v_st in zip(m_rs, u0_sts, v_sts)]
    wuvs = [jnp.concatenate([fold_heads(w_st), fold_heads(u0_st), v[rs, ls], zf], 0).T.astype(BF16)
            for rs, ls, w_st, u0_st in zip(rows, lanes, w_sts, u0_sts)]
    ends = [jnp.concatenate([jnp.concatenate([bh[rs, ls], zq], 1), jnp.concatenate([zq, bh[rs, ls]], 1),
                             jnp.concatenate([zq, kh[rs, ls]], 1), jnp.concatenate([zq, zq], 1)], 0)
            for rs, ls in zip(rows, lanes)]
    ghs = [_dot(wuv, end) for wuv, end in zip(wuvs, ends)]
    fin = {}
    for u, rhat, yc, gh in zip(units, rhats, ycs, ghs):
        g = jnp.where(same_head, gh[:, :QUAD], 0.0).astype(BF16)
        hc = jnp.where(same_head, gh[:, QUAD:], 0.0)
        fin[u] = (rhat.astype(BF16), yc, g, hc)

    order = range(nchunk - 1, -1, -1) if reverse else range(nchunk)
    states = [st_ref[q] for q in range(nquad)]
    for c in order:
        rs = slice(c * CHUNK, (c + 1) * CHUNK)
        for q in range(nquad):
            ls = slice(q * QUAD, (q + 1) * QUAD)
            rhat, yc, g, hc = fin[(c, q)]
            st = states[q]
            stb = st.astype(BF16)
            y_ref[rs, ls] = _dot_nt(rhat, stb) + yc
            states[q] = st * dec_all[c * CHUNK:c * CHUNK + 1, ls] + _dot(stb, g) + hc
    for q in range(nquad):
        st_ref[q] = states[q]

    ones = ones_ref[...]
    inv_n = 1.0 / HEAD
    y = y_ref[...]
    mean = _split_dot(y, ones) * inv_n
    d = y - mean
    var = _split_dot(d * d, ones) * inv_n
    yn = d * lax.rsqrt(var + GN_EPS) * gng_ref[...] + gnb_ref[...]
    bonus = _split_dot(r * kd * rk_ref[...], ones) * v
    o_ref[...] = (yn + bonus) * g_ref[...]


def _scan(r, v, kk, lw, kd, bb, g, rk, gn_g, gn_b, ts, ones_bd, direction):
    b, tall, c = r.shape
    nt = tall // TB
    reverse = direction == 1
    if reverse:
        tmap = lambda t: jnp.where(t == 0, 0, nt - t)
    else:
        tmap = lambda t: t
    shared = pl.BlockSpec((None, TB, c), lambda i, t: (i, tmap(t), 0))
    perdir = pl.BlockSpec((None, None, TB, c), lambda i, t: (direction, i, tmap(t), 0))
    const2 = lambda i, t: (0, 0)
    return pl.pallas_call(
        functools.partial(_scan_kernel, reverse=reverse),
        grid=(b, nt),
        in_specs=[shared, shared, shared, perdir, perdir, perdir, perdir,
                  pl.BlockSpec((None, 1, c), lambda i, t: (direction, 0, 0)),
                  pl.BlockSpec((1, c), const2),
                  pl.BlockSpec((1, c), const2),
                  pl.BlockSpec(ts.shape, const2),
                  pl.BlockSpec((c, c), const2)],
        out_specs=pl.BlockSpec((None, TB, c), lambda i, t: (i, tmap(t), 0)),
        out_shape=jax.ShapeDtypeStruct((b, tall, c), F32),
        scratch_shapes=[pltpu.VMEM((c // QUAD, QUAD, QUAD), F32),
                        pltpu.VMEM((TB, c), F32)],
        compiler_params=_cp(("parallel", "arbitrary")),
        name="scan_rev" if reverse else "scan_fwd",
    )(r, v, kk, lw, kd, bb, g, rk, gn_g, gn_b, ts, ones_bd)


def _mm_kernel(a_ref, b_ref, o_ref, acc_ref):
    k = pl.program_id(2)

    @pl.when(k == 0)
    def _():
        acc_ref[...] = jnp.zeros_like(acc_ref)

    acc_ref[...] += _dot(a_ref[...], b_ref[...])

    @pl.when(k == pl.num_programs(2) - 1)
    def _():
        o_ref[...] = acc_ref[...]


def _matmul(a, b, tm, tn, tk):
    m, kd = a.shape
    n = b.shape[1]
    tm, tn, tk = min(tm, m), min(tn, n), min(tk, kd)
    return pl.pallas_call(
        _mm_kernel,
        grid=(m // tm, n // tn, kd // tk),
        in_specs=[pl.BlockSpec((tm, tk), lambda i, j, k: (i, k)),
                  pl.BlockSpec((tk, tn), lambda i, j, k: (k, j))],
        out_specs=pl.BlockSpec((tm, tn), lambda i, j, k: (i, j)),
        out_shape=jax.ShapeDtypeStruct((m, n), F32),
        scratch_shapes=[pltpu.VMEM((tm, tn), F32)],
        compiler_params=_cp(("parallel", "parallel", "arbitrary")),
        name="dft",
    )(a, b)


def _dft_matrix(seq):
    hi = seq // 128
    k = np.arange(seq, dtype=np.int64)[:, None]
    a_ang = 2.0 * np.pi * ((k * np.arange(hi)[None, :] * 128) % seq) / seq
    b_ang = 2.0 * np.pi * ((k * np.arange(128)[None, :]) % seq) / seq
    ca = jnp.asarray(np.cos(a_ang), F32)[:, :, None]
    sa = jnp.asarray(np.sin(a_ang), F32)[:, :, None]
    cb = jnp.asarray(np.cos(b_ang), F32)[:, None, :]
    sb = jnp.asarray(np.sin(b_ang), F32)[:, None, :]
    cos = (ca * cb - sa * sb).reshape(seq, seq)
    sin = (sa * cb + ca * sb).reshape(seq, seq)
    return jnp.concatenate([cos, sin], 1).astype(BF16)


def _out_kernel(x_ref, o0_ref, o1_ref, fn_ref, g0_ref, b0_ref, gt_ref, sc_ref, sh_ref, wt_ref, wb_ref, bf_ref,
                g1_ref, b1_ref, wr_ref, br_ref, tri_ref, one_ref,
                x1_o, h2_o, idx_o, gate_o, rank_o, cnt_o):
    xn = _ln(x_ref[...], g0_ref[...], b0_ref[...])
    rw = (o0_ref[...] + o1_ref[...]).astype(BF16)
    fn = (fn_ref[...] + bf_ref[...]).astype(BF16)
    mo = _dot(rw, wt_ref[...]) + _dot(fn, wb_ref[...])
    x1 = _ln(DEEPNORM_ALPHA * xn + gt_ref[...] * mo, g1_ref[...], b1_ref[...])
    h2 = x1 * (1.0 + sc_ref[...]) + sh_ref[...]
    x1_o[...] = x1
    h2_o[...] = h2

    vals = _dot_nt(wr_ref[...], h2, precision=HIGHEST) + br_ref[...]
    ne = vals.shape[0]
    rowid = lax.broadcasted_iota(jnp.int32, vals.shape, 0)
    sels, tops, idxs = [], [], []
    for _ in range(TOP_K):
        mx = jnp.max(vals, axis=0, keepdims=True)
        ix = jnp.min(jnp.where(vals == mx, rowid, ne), axis=0, keepdims=True)
        sel = rowid == ix
        vals = jnp.where(sel, -jnp.inf, vals)
        sels.append(sel)
        tops.append(mx)
        idxs.append(ix)
    ex = [jnp.exp(m - tops[0]) for m in tops]
    den = ex[0] + ex[1] + ex[2] + ex[3]
    gate_o[...] = jnp.concatenate([e / den for e in ex], 0)
    idx_o[...] = jnp.concatenate(idxs, 0)
    onehot = jnp.zeros(vals.shape, F32)
    for sel in sels:
        onehot = onehot + jnp.where(sel, 1.0, 0.0)
    ohb = onehot.astype(BF16)
    before = _dot(ohb, tri_ref[...])
    rank_o[...] = jnp.concatenate(
        [jnp.sum(jnp.where(sel, before, 0.0), axis=0, keepdims=True) for sel in sels], 0).astype(jnp.int32)
    cnt_o[...] = _dot(ohb, one_ref[...])


def _outproj(x, o0, o1, fno, ln0_g, ln0_b, gt1, sc2, sh2, w_top, w_bot, b_fno, ln1_g, ln1_b, w_rt, b_r, tri, ones_col):
    b, seq, d = x.shape
    c = o0.shape[-1]
    nt = seq // TB
    ntile = b * nt
    ne = w_rt.shape[0]
    const2 = lambda i, t: (0, 0)
    vec = pl.BlockSpec((1, d), const2)
    mod = pl.BlockSpec((None, 1, d), lambda i, t: (i, 0, 0))
    tok = lambda i, t: (0, i * nt + t)
    return pl.pallas_call(
        _out_kernel,
        grid=(b, nt),
        in_specs=[pl.BlockSpec((None, TB, d), lambda i, t: (i, t, 0)),
                  pl.BlockSpec((None, TB, c), lambda i, t: (i, t + 1, 0)),
                  pl.BlockSpec((None, TB, c), lambda i, t: (i, t + 1, 0)),
                  pl.BlockSpec((TB, c), lambda i, t: (t, i)),
                  vec, vec, mod, mod, mod,
                  pl.BlockSpec((c, d), const2),
                  pl.BlockSpec((c, d), const2),
                  pl.BlockSpec((1, c), const2),
                  vec, vec,
                  pl.BlockSpec((ne, d), const2),
                  pl.BlockSpec((ne, 1), const2),
                  pl.BlockSpec((TB, TB), const2),
                  pl.BlockSpec((TB, 128), const2)],
        out_specs=[pl.BlockSpec((None, TB, d), lambda i, t: (i, t, 0)),
                   pl.BlockSpec((TB, d), lambda i, t: (i * nt + t, 0)),
                   pl.BlockSpec((TOP_K, TB), tok),
                   pl.BlockSpec((TOP_K, TB), tok),
                   pl.BlockSpec((TOP_K, TB), tok),
                   pl.BlockSpec((None, ne, 128), lambda i, t: (i * nt + t, 0, 0))],
        out_shape=[jax.ShapeDtypeStruct((b, seq, d), F32),
                   jax.ShapeDtypeStruct((b * seq, d), F32),
                   jax.ShapeDtypeStruct((TOP_K, b * seq), jnp.int32),
                   jax.ShapeDtypeStruct((TOP_K, b * seq), F32),
                   jax.ShapeDtypeStruct((TOP_K, b * seq), jnp.int32),
                   jax.ShapeDtypeStruct((ntile, ne, 128), F32)],
        compiler_params=_cp(("parallel", "arbitrary")),
        name="outproj",
    )(x, o0, o1, fno, ln0_g, ln0_b, gt1, sc2, sh2, w_top, w_bot, b_fno, ln1_g, ln1_b, w_rt, b_r, tri, ones_col)


def _slot_kernel(idx_ref, rank_ref, base_ref, o_ref):
    idx = idx_ref[...]
    base = base_ref[...]
    rowid = lax.broadcasted_iota(jnp.int32, (base.shape[0], idx.shape[1]), 0)
    rows = [jnp.sum(jnp.where(rowid == idx[k:k + 1], base, 0), axis=0, keepdims=True) for k in range(TOP_K)]
    o_ref[...] = rank_ref[...] + jnp.concatenate(rows, 0)


def _slots(idx, rank, base):
    ntile, ne, _ = base.shape
    return pl.pallas_call(
        _slot_kernel,
        grid=(ntile,),
        in_specs=[pl.BlockSpec((TOP_K, TB), lambda i: (0, i)),
                  pl.BlockSpec((TOP_K, TB), lambda i: (0, i)),
                  pl.BlockSpec((None, ne, 1), lambda i: (i, 0, 0))],
        out_specs=pl.BlockSpec((None, TOP_K, TB), lambda i: (i, 0, 0)),
        out_shape=jax.ShapeDtypeStruct((ntile, TOP_K, TB), jnp.int32),
        compiler_params=_cp(("parallel",)),
        name="slot",
    )(idx, rank, base)


def _row_copy(src, src_row, dst, dst_row, sem):
    return pltpu.make_async_copy(src.at[pl.ds(src_row, 1)], dst.at[pl.ds(dst_row, 1)], sem)


def _dispatch_kernel(pad_ref, slot_hbm, h_ref, xs_ref, slot_smem, zrow_ref, sem, ssem):
    i = pl.program_id(0)
    cp = pltpu.make_async_copy(slot_hbm.at[i], slot_smem, ssem)
    cp.start()
    cp.wait()

    def issue(j, carry):
        for k in range(TOP_K):
            _row_copy(h_ref, j, xs_ref, slot_smem[k, j], sem).start()
        return carry

    lax.fori_loop(0, TB, issue, 0)

    def drain(j, carry):
        for k in range(TOP_K):
            _row_copy(h_ref, 0, xs_ref, 0, sem).wait()
        return carry

    lax.fori_loop(0, TB, drain, 0)

    @pl.when(i == pl.num_programs(0) - 1)
    def _():
        zrow_ref[...] = jnp.zeros_like(zrow_ref)

        def per_expert(e, carry):
            start = pad_ref[e]
            n = pad_ref[N_EXPERTS + e]

            def zi(q, c2):
                _row_copy(zrow_ref, 0, xs_ref, start + q, sem).start()
                return c2

            lax.fori_loop(0, n, zi, 0)

            def zw(q, c2):
                _row_copy(zrow_ref, 0, xs_ref, 0, sem).wait()
                return c2

            lax.fori_loop(0, n, zw, 0)
            return carry

        lax.fori_loop(0, N_EXPERTS, per_expert, 0)


def _dispatch(padinfo, slot, h2, n_slots):
    n, d = h2.shape
    ntile = n // TB
    return pl.pallas_call(
        _dispatch_kernel,
        grid_spec=pltpu.PrefetchScalarGridSpec(
            num_scalar_prefetch=1,
            grid=(ntile,),
            in_specs=[pl.BlockSpec(memory_space=pl.ANY),
                      pl.BlockSpec((TB, d), lambda i, pad: (i, 0))],
            out_specs=pl.BlockSpec(memory_space=pl.ANY),
            scratch_shapes=[pltpu.SMEM((TOP_K, TB), jnp.int32),
                            pltpu.VMEM((1, d), F32),
                            pltpu.SemaphoreType.DMA,
                            pltpu.SemaphoreType.DMA]),
        out_shape=jax.ShapeDtypeStruct((n_slots, d), F32),
        compiler_params=_cp(("arbitrary",)),
        name="dispatch",
    )(padinfo, slot, h2)


def _expert_kernel(be_ref, nu_ref, x_ref, w1_ref, b1_ref, w2_ref, b2_ref, perm_ref, o_ref, w1b_ref, w2b_ref):
    i = pl.program_id(0)
    prev = be_ref[jnp.maximum(i - 1, 0)]
    changed = jnp.logical_or(i == 0, be_ref[i] != prev)
    dff2 = w1_ref.shape[1]
    nblk = dff2 // 256

    @pl.when(changed)
    def _():
        perm = perm_ref[...]
        for j in range(nblk):
            cs = slice(j * 256, (j + 1) * 256)
            w1b_ref[:, cs] = _dot(w1_ref[:, cs].astype(BF16), perm).astype(BF16)
        w2b_ref[...] = w2_ref[...].astype(BF16)

    @pl.when(i < nu_ref[0])
    def _():
        u = _dot(x_ref[...].astype(BF16), w1b_ref[...]) + b1_ref[...]
        acts = []
        for j in range(nblk):
            glu = jnp.minimum(u[:, j * 256:j * 256 + 128], SWIGLU_LIMIT)
            lin = jnp.clip(u[:, j * 256 + 128:(j + 1) * 256], -SWIGLU_LIMIT, SWIGLU_LIMIT)
            acts.append(glu * _sigmoid(SWIGLU_ALPHA * glu) * (lin + 1.0))
        act = jnp.concatenate(acts, 1).astype(BF16)
        o_ref[...] = _dot(act, w2b_ref[...]) + b2_ref[...]

    @pl.when(i >= nu_ref[0])
    def _():
        o_ref[...] = jnp.zeros_like(o_ref)


def _experts(block_e, n_used, xs, w1, b1p, w2, b2, perm):
    n_slots, d = xs.shape
    ne, _, dff2 = w1.shape
    dff = w2.shape[1]
    nblocks = n_slots // EBLK
    return pl.pallas_call(
        _expert_kernel,
        grid_spec=pltpu.PrefetchScalarGridSpec(
            num_scalar_prefetch=2,
            grid=(nblocks,),
            in_specs=[pl.BlockSpec((EBLK, d), lambda i, be, nu: (jnp.minimum(i, nu[0] - 1), 0)),
                      pl.BlockSpec((None, d, dff2), lambda i, be, nu: (be[i], 0, 0)),
                      pl.BlockSpec((None, 1, dff2), lambda i, be, nu: (be[i], 0, 0)),
                      pl.BlockSpec((None, dff, d), lambda i, be, nu: (be[i], 0, 0)),
                      pl.BlockSpec((None, 1, d), lambda i, be, nu: (be[i], 0, 0)),
                      pl.BlockSpec((256, 256), lambda i, be, nu: (0, 0))],
            out_specs=pl.BlockSpec((EBLK, d), lambda i, be, nu: (i, 0)),
            scratch_shapes=[pltpu.VMEM((d, dff2), BF16),
                            pltpu.VMEM((dff, d), BF16)]),
        out_shape=jax.ShapeDtypeStruct((n_slots, d), F32),
        compiler_params=_cp(("arbitrary",)),
        name="experts",
    )(block_e, n_used, xs, w1, b1p, w2, b2, perm)


def _combine_kernel(slot_hbm, ys_ref, gate_ref, x1_ref, gt_ref, g_ref, b_ref, eye_ref, o_ref, slot_smem, buf_ref, sem, ssem):
    i = pl.program_id(0)
    cp = pltpu.make_async_copy(slot_hbm.at[i], slot_smem, ssem)
    cp.start()
    cp.wait()

    def issue(j, carry):
        for k in range(TOP_K):
            _row_copy(ys_ref, slot_smem[k, j], buf_ref.at[k], j, sem).start()
        return carry

    lax.fori_loop(0, TB, issue, 0)

    def drain(j, carry):
        for k in range(TOP_K):
            _row_copy(ys_ref, 0, buf_ref.at[k], 0, sem).wait()
        return carry

    lax.fori_loop(0, TB, drain, 0)

    gt = _dot_nt(eye_ref[...], gate_ref[...], precision=HIGHEST)
    f = gt[:, 0:1] * buf_ref[0]
    for k in range(1, TOP_K):
        f = f + gt[:, k:k + 1] * buf_ref[k]
    o_ref[...] = _ln(DEEPNORM_ALPHA * x1_ref[...] + gt_ref[...] * f, g_ref[...], b_ref[...])


def _combine(slot, ys, gates, x1, gt2, ln_g, ln_b, eye, tiles_per_batch):
    n, d = x1.shape
    ntile = n // TB
    return pl.pallas_call(
        _combine_kernel,
        grid=(ntile,),
        in_specs=[pl.BlockSpec(memory_space=pl.ANY),
                  pl.BlockSpec(memory_space=pl.ANY),
                  pl.BlockSpec((TOP_K, TB), lambda i: (0, i)),
                  pl.BlockSpec((TB, d), lambda i: (i, 0)),
                  pl.BlockSpec((None, 1, d), lambda i: (i // tiles_per_batch, 0, 0)),
                  pl.BlockSpec((1, d), lambda i: (0, 0)),
                  pl.BlockSpec((1, d), lambda i: (0, 0)),
                  pl.BlockSpec((TB, TB), lambda i: (0, 0))],
        out_specs=pl.BlockSpec((TB, d), lambda i: (i, 0)),
        out_shape=jax.ShapeDtypeStruct((n, d), F32),
        scratch_shapes=[pltpu.SMEM((TOP_K, TB), jnp.int32),
                        pltpu.VMEM((TOP_K, TB, d), F32),
                        pltpu.SemaphoreType.DMA,
                        pltpu.SemaphoreType.DMA],
        compiler_params=_cp(("arbitrary",)),
        name="combine",
    )(slot, ys, gates, x1, gt2, ln_g, ln_b, eye)


def _scan_tables(reverse):
    t = np.arange(TB)
    same = (t[:, None] // CHUNK) == (t[None, :] // CHUNK)
    tri = (t[None, :] >= t[:, None]) if reverse else (t[None, :] <= t[:, None])
    return jnp.asarray(np.concatenate([same & tri, same], 0).astype(np.float32), BF16)


def _head_ones(c):
    h = np.arange(c) // HEAD
    return jnp.asarray((h[:, None] == h[None, :]).astype(np.float32), BF16)


def _deinterleave_perm():
    p = np.zeros((256, 256), np.float32)
    j = np.arange(128)
    p[2 * j, j] = 1.0
    p[2 * j + 1, 128 + j] = 1.0
    return jnp.asarray(p, BF16)


def kernel(x, c, ctx, c_ctx, ln0_g, ln0_b, w_ada, b_ada, w_in, mu_shift, w0, w2_decay, a0, a2_iclr, g2_gate, r_k, k_k, k_a, gn_g, gn_b, w_fno, b_fno, w_out, ln1_g, ln1_b, w_router, b_router, w1, b1, w2, b2, ln2_g, ln2_b):
    b, seq, d = x.shape
    ctx_len = ctx.shape[1]
    assert ctx_len == TB and seq % TB == 0 and w_ada.shape[0] == 1
    n_dir, cw = w0.shape[1], w0.shape[2]
    fgroups, gw = w_fno.shape[1], w_fno.shape[2]
    fw = fgroups * gw
    shift_w = mu_shift.shape[1]
    nd, na, ng = w2_decay.shape[2], a2_iclr.shape[2], g2_gate.shape[2]
    assert n_dir == 2 and shift_w == 3 * cw + 2 * (nd + na + ng)
    ne = w_router.shape[2]
    n_tok = b * seq
    row = lambda a: a.reshape(1, -1)

    rows = -(-(b + 1) // 8) * 8
    cc = jnp.zeros((rows, d), F32).at[:b].set(c).at[b].set(c_ctx)
    mod = _ada(cc, w_ada[0], b_ada[0])
    sh1, sc1, gt1, sh2, sc2, gt2 = [mod[:b, i * d:(i + 1) * d] for i in range(6)]
    sh1c, sc1c = mod[b, :d], mod[b, d:2 * d]
    scsel = jnp.stack([jnp.broadcast_to(sc1c, (b, d)), sc1], 1)[:, :, None, :]
    shsel = jnp.stack([jnp.broadcast_to(sh1c, (b, d)), sh1], 1)[:, :, None, :]
    mod3 = lambda a: a[:, None, :]

    cidx = np.arange(gw)
    ang = 2.0 * np.pi * ((cidx[:, None] * cidx[None, :]) % gw) / gw
    norm = 1.0 / math.sqrt(seq * gw)
    csc = jnp.asarray(np.stack([np.cos(ang) * norm, -np.sin(ang) * norm]), F32)
    win_f3 = w_in[0][:, shift_w:].reshape(d, fgroups, gw).transpose(1, 0, 2)
    wfc, wfs = _fold(csc, w_fno[0], win_f3)
    unf = lambda a: a.transpose(1, 0, 2).reshape(d, fw)
    w3 = jnp.concatenate([w_in[0][:, :shift_w], unf(wfc), unf(wfs)], 1).astype(BF16)

    s_all, fcat = _inproj(ctx, x, scsel, shsel, row(ln0_g), row(ln0_b), w3, shift_w, fw)

    def pad_dir(w, width):
        out = jnp.zeros((2, 2 * width, cw), F32)
        return out.at[0, :width].set(w[0]).at[1, width:].set(w[1]).astype(BF16)

    ones_bd = _head_ones(cw)
    r, v, kk, lw, kd, bb, g = _prep(
        s_all, row(mu_shift[0]), w0[0][:, None, :], pad_dir(w2_decay[0], nd), a0[0][:, None, :],
        pad_dir(a2_iclr[0], na), pad_dir(g2_gate[0], ng), row(k_k[0]), row(k_a[0]), ones_bd, cw)

    rk = r_k[0].reshape(2, 1, cw)
    outs = [_scan(r, v, kk, lw, kd, bb, g, rk, row(gn_g[0]), row(gn_b[0]), _scan_tables(dr == 1), ones_bd, dr)
            for dr in range(2)]

    fno = _matmul(_dft_matrix(seq), fcat.reshape(2 * seq, b * fw), 1024, 2048, 1024)

    ids = np.arange(TB)
    tri = jnp.asarray((ids[:, None] < ids[None, :]).astype(np.float32), BF16)
    ones_col = jnp.ones((TB, 128), BF16)
    wo = w_out[0].astype(BF16)
    x1, h2, idx, gates, rank, cnt = _outproj(
        x, outs[0], outs[1], fno, row(ln0_g), row(ln0_b), mod3(gt1), mod3(sc2), mod3(sh2), wo[:cw], wo[cw:],
        row(b_fno[0]), row(ln1_g[0]), row(ln1_b[0]), w_router[0].T, b_router[0].reshape(ne, 1), tri, ones_col)

    ntile = n_tok // TB
    cnt_t = cnt[:, :, 0].astype(jnp.int32)
    counts = jnp.sum(cnt_t, 0)
    padded = (counts + EBLK - 1) // EBLK * EBLK
    pends = jnp.cumsum(padded)
    pstarts = pends - padded
    base = (pstarts[None, :] + jnp.cumsum(cnt_t, 0) - cnt_t)[:, :, None]
    nblocks = (n_tok * TOP_K) // EBLK + ne
    n_slots = nblocks * EBLK
    block_start = jnp.arange(nblocks, dtype=jnp.int32) * EBLK
    block_e = jnp.minimum(jnp.sum((pends[None, :] <= block_start[:, None]).astype(jnp.int32), 1), ne - 1)
    n_used = (pends[-1:] // EBLK).astype(jnp.int32)
    padinfo = jnp.concatenate([pstarts + counts, padded - counts]).astype(jnp.int32)

    slot = _slots(idx, rank, base)
    xs = _dispatch(padinfo, slot, h2, n_slots)
    b1p = b1[0].reshape(ne, -1, 128, 2).transpose(0, 1, 3, 2).reshape(ne, 1, -1)
    ys = _experts(block_e, n_used, xs, w1[0], b1p, w2[0], b2[0][:, None, :], _deinterleave_perm())
    out = _combine(slot, ys, gates, x1.reshape(n_tok, d), mod3(gt2), row(ln2_g[0]), row(ln2_b[0]),
                   jnp.eye(TB, dtype=F32), seq // TB)
    return out.reshape(b, seq, d)
```

```python
import functools
import math

import numpy as np
import jax
import jax.numpy as jnp
from jax import lax
from jax.experimental import pallas as pl
from jax.experimental.pallas import tpu as pltpu

F32 = jnp.float32
BF16 = jnp.bfloat16
HIGHEST = lax.Precision.HIGHEST

GRID_W = 64
HEAD = 64
CHUNK = 64
TB = 256
N_EXPERTS = 32
TOP_K = 4
EBLK = 256
SWIGLU_LIMIT = 7.0
SWIGLU_ALPHA = 1.702
LN_EPS = 1e-5
GN_EPS = 64e-5
DEEPNORM_ALPHA = 2.0 ** 0.25
DECAY_SCALE = math.exp(-0.5)
LANES = 128
VMEM_LIMIT = 56 * 1024 * 1024


def _cp(sem, vmem=VMEM_LIMIT):
    return pltpu.CompilerParams(dimension_semantics=sem, vmem_limit_bytes=vmem)


def _sigmoid(x):
    return 1.0 / (1.0 + jnp.exp(-x))


def _ln(u, g, b):
    mean = jnp.mean(u, -1, keepdims=True)
    d = u - mean
    var = jnp.mean(d * d, -1, keepdims=True)
    return d * lax.rsqrt(var + LN_EPS) * g + b


def _dot(a, b):
    return jnp.dot(a, b, preferred_element_type=F32)


def _dot_nt(a, b, precision=None):
    return lax.dot_general(a, b, (((1,), (1,)), ((), ())), precision=precision, preferred_element_type=F32)


def _split_dot(x, w):
    hi = x.astype(BF16)
    lo = (x - hi.astype(F32)).astype(BF16)
    return _dot(hi, w) + _dot(lo, w)


def _load_token_rows(ref, n, pieces):
    return jnp.concatenate([ref[pl.ds(j, n, stride=pieces), :] for j in range(pieces)], 1)


def _store_token_rows(ref, val):
    n, width = val.shape
    pieces = width // LANES
    for j in range(pieces):
        ref[pl.ds(j, n, stride=pieces), :] = val[:, j * LANES:(j + 1) * LANES]


def _ada_kernel(c_ref, w_ref, b_ref, o_ref):
    c = c_ref[...]
    s = c * _sigmoid(c)
    o_ref[...] = jnp.dot(s, w_ref[...], precision=HIGHEST, preferred_element_type=F32) + b_ref[...]


def _ada(cc, w_ada, b_ada):
    rows, d = cc.shape
    n = w_ada.shape[1]
    tn = 1024
    return pl.pallas_call(
        _ada_kernel,
        grid=(n // tn,),
        in_specs=[pl.BlockSpec((rows, d), lambda j: (0, 0)),
                  pl.BlockSpec((d, tn), lambda j: (0, j)),
                  pl.BlockSpec((1, tn), lambda j: (0, j))],
        out_specs=pl.BlockSpec((rows, tn), lambda j: (0, j)),
        out_shape=jax.ShapeDtypeStruct((rows, n), F32),
        compiler_params=_cp(("parallel",)),
        name="ada",
    )(cc, w_ada, b_ada.reshape(1, n))


def _fold_kernel(cs_ref, wf_ref, win_ref, oc_ref, os_ref):
    wf = wf_ref[...]
    mc = jnp.dot(cs_ref[0], wf, precision=HIGHEST, preferred_element_type=F32)
    ms = jnp.dot(cs_ref[1], wf, precision=HIGHEST, preferred_element_type=F32)
    w = win_ref[...]
    oc_ref[...] = jnp.dot(w, mc, precision=HIGHEST, preferred_element_type=F32)
    os_ref[...] = jnp.dot(w, ms, precision=HIGHEST, preferred_element_type=F32)


def _fold(csc, w_fno, win_f3):
    g, d, gw = win_f3.shape
    return pl.pallas_call(
        _fold_kernel,
        grid=(g,),
        in_specs=[pl.BlockSpec((2, gw, gw), lambda i: (0, 0, 0)),
                  pl.BlockSpec((None, gw, gw), lambda i: (i, 0, 0)),
                  pl.BlockSpec((None, d, gw), lambda i: (i, 0, 0))],
        out_specs=[pl.BlockSpec((None, d, gw), lambda i: (i, 0, 0)),
                   pl.BlockSpec((None, d, gw), lambda i: (i, 0, 0))],
        out_shape=[jax.ShapeDtypeStruct((g, d, gw), F32)] * 2,
        compiler_params=_cp(("parallel",)),
        name="fold",
    )(csc, w_fno, win_f3)


def _in_kernel(ctx_ref, x_ref, sc_ref, sh_ref, g_ref, b_ref, w_ref, s_ref, f_ref, *, shift_w, fw):
    xin = jnp.where(pl.program_id(1) == 0, ctx_ref[...], x_ref[...])
    xn = _ln(xin, g_ref[...], b_ref[...])
    h = xn * (1.0 + sc_ref[...]) + sh_ref[...]
    p = _dot(h.astype(BF16), w_ref[...])
    s_ref[...] = p[:, :shift_w]
    f_ref[0] = p[:, shift_w:shift_w + fw].astype(BF16)
    f_ref[1] = p[:, shift_w + fw:].astype(BF16)


def _inproj(ctx, x, scsel, shsel, ln_g, ln_b, w3, shift_w, fw):
    b, seq, d = x.shape
    tall = ctx.shape[1] + seq
    nt = tall // TB
    wn = w3.shape[1]
    return pl.pallas_call(
        functools.partial(_in_kernel, shift_w=shift_w, fw=fw),
        grid=(b, nt),
        in_specs=[pl.BlockSpec((None, TB, d), lambda i, t: (i, 0, 0)),
                  pl.BlockSpec((None, TB, d), lambda i, t: (i, jnp.maximum(t - 1, 0), 0)),
                  pl.BlockSpec((None, None, 1, d), lambda i, t: (i, jnp.minimum(t, 1), 0, 0)),
                  pl.BlockSpec((None, None, 1, d), lambda i, t: (i, jnp.minimum(t, 1), 0, 0)),
                  pl.BlockSpec((1, d), lambda i, t: (0, 0)),
                  pl.BlockSpec((1, d), lambda i, t: (0, 0)),
                  pl.BlockSpec((d, wn), lambda i, t: (0, 0))],
        out_specs=[pl.BlockSpec((None, TB, shift_w), lambda i, t: (i, t, 0)),
                   pl.BlockSpec((2, TB, fw), lambda i, t: (0, jnp.maximum(t - 1, 0), i))],
        out_shape=[jax.ShapeDtypeStruct((b, tall, shift_w), F32),
                   jax.ShapeDtypeStruct((2, seq, b * fw), BF16)],
        compiler_params=_cp(("parallel", "arbitrary")),
        name="inproj",
    )(ctx, x, scsel, shsel, ln_g, ln_b, w3)


def _prep_kernel(s_ref, sp_ref, sn_ref, mu_ref, w0_ref, w2d_ref, a0_ref, a2_ref, g2_ref, kkw_ref, ka_ref, ones_ref,
                 r_o, v_o, kk_o, lw_o, kd_o, bb_o, g_o, *, c):
    t = pl.program_id(1)
    nt = pl.num_programs(1)
    s = s_ref[...]
    idx = lax.broadcasted_iota(jnp.int32, (TB, 1), 0)
    col = idx & (GRID_W - 1)
    is_ctx = t == 0
    lmask = jnp.where(is_ctx, idx, col) == 0
    rmask = jnp.where(is_ctx, idx - (TB - 1), col - (GRID_W - 1)) == 0
    left = jnp.where(lmask, 0.0, pltpu.roll(s, 1, 0))
    right = jnp.where(rmask, 0.0, pltpu.roll(s, TB - 1, 0))
    up = jnp.concatenate([jnp.where(t == 1, 0.0, sp_ref[...]), s[:TB - GRID_W]], 0)
    down = jnp.concatenate([s[GRID_W:], jnp.where(t == nt - 1, 0.0, sn_ref[...])], 0)
    ud = jnp.where(is_ctx, 0.0, up + down)
    sh = (ud + left + right) * jnp.where(is_ctx, 0.5, 0.25)
    m = s + mu_ref[...] * (sh - s)

    r = m[:, :c]
    k = m[:, c:2 * c]
    v = m[:, 2 * c:3 * c]
    o = 3 * c
    nd = w2d_ref.shape[1]
    na = a2_ref.shape[1]
    ng = g2_ref.shape[1]
    wd = jnp.tanh(m[:, o:o + nd]).astype(BF16)
    ad = m[:, o + nd:o + nd + na].astype(BF16)
    gd = _sigmoid(m[:, o + nd + na:o + nd + na + ng]).astype(BF16)
    kk = k * kkw_ref[...]
    ss = _split_dot(kk * kk, ones_ref[...])
    kk = kk / jnp.maximum(jnp.sqrt(ss), 1e-12)
    r_o[...] = r
    v_o[...] = v
    kk_o[...] = kk
    for d in range(2):
        wl = w0_ref[d] + _dot(wd, w2d_ref[d])
        lw_o[d] = -DECAY_SCALE * _sigmoid(wl)
        a = _sigmoid(a0_ref[d] + _dot(ad, a2_ref[d]))
        g_o[d] = _dot(gd, g2_ref[d])
        kd_o[d] = k * (1.0 + (a - 1.0) * ka_ref[...])
        bb_o[d] = kk * a


def _prep(s_all, mu, w0, w2d, a0, a2, g2, k_k, k_a, ones_bd, c):
    b, tall, sw = s_all.shape
    nt = tall // TB
    nhb = tall // GRID_W
    hb = TB // GRID_W
    const2 = lambda i, t: (0, 0)
    const3 = lambda i, t: (0, 0, 0)
    o1 = pl.BlockSpec((None, TB, c), lambda i, t: (i, t, 0))
    o2 = pl.BlockSpec((2, None, TB, c), lambda i, t: (0, i, t, 0))
    s1 = jax.ShapeDtypeStruct((b, tall, c), F32)
    s2 = jax.ShapeDtypeStruct((2, b, tall, c), F32)
    return pl.pallas_call(
        functools.partial(_prep_kernel, c=c),
        grid=(b, nt),
        in_specs=[pl.BlockSpec((None, TB, sw), lambda i, t: (i, t, 0)),
                  pl.BlockSpec((None, GRID_W, sw), lambda i, t: (i, jnp.maximum(t * hb - 1, 0), 0)),
                  pl.BlockSpec((None, GRID_W, sw), lambda i, t: (i, jnp.minimum(t * hb + hb, nhb - 1), 0)),
                  pl.BlockSpec((1, sw), const2),
                  pl.BlockSpec(w0.shape, const3),
                  pl.BlockSpec(w2d.shape, const3),
                  pl.BlockSpec(a0.shape, const3),
                  pl.BlockSpec(a2.shape, const3),
                  pl.BlockSpec(g2.shape, const3),
                  pl.BlockSpec((1, c), const2),
                  pl.BlockSpec((1, c), const2),
                  pl.BlockSpec((c, c), const2)],
        out_specs=[o1, o1, o1, o2, o2, o2, o2],
        out_shape=[s1, s1, s1, s2, s2, s2, s2],
        compiler_params=_cp(("parallel", "arbitrary")),
        name="prep",
    )(s_all, s_all, s_all, mu, w0, w2d, a0, a2, g2, k_k, k_a, ones_bd)


QUAD = 4 * HEAD


def _unit_triangular_inverses(ns):
    ti = lax.broadcasted_iota(jnp.int32, ns[0].shape, 0)
    tj = lax.broadcasted_iota(jnp.int32, ns[0].shape, 1)

    def same_block(shift):
        return (ti >> shift) == (tj >> shift)

    zero = jnp.zeros_like(ns[0])
    n8 = [jnp.where(same_block(3), n, zero) for n in ns]
    n8s = [_dot(a, a).astype(BF16) for a in n8]
    n8q = [_dot(a, a).astype(BF16) for a in n8s]
    xs = [jnp.where(ti == tj, jnp.ones_like(a), a) for a in n8]
    xs = [(x.astype(F32) + _dot(x, a)).astype(BF16) for x, a in zip(xs, n8s)]
    xs = [(x.astype(F32) + _dot(x, a)).astype(BF16) for x, a in zip(xs, n8q)]
    for shift in (3, 4, 5):
        offs = [jnp.where(same_block(shift + 1), jnp.where(same_block(shift), zero, n), zero) for n in ns]
        fs = [_dot(x, off).astype(BF16) for x, off in zip(xs, offs)]
        xs = [(x.astype(F32) + _dot(f, x)).astype(BF16) for x, f in zip(xs, fs)]
    return xs


def _scan_kernel(r_ref, v_ref, kk_ref, lw_ref, kd_ref, bb_ref, g_ref, rk_ref, gng_ref, gnb_ref, ts_ref, ones_ref,
                 o_ref, st_ref, y_ref, *, reverse):
    t = pl.program_id(1)
    width = r_ref.shape[-1]
    nquad = width // QUAD
    nchunk = TB // CHUNK
    nh = QUAD // HEAD

    @pl.when(t == 0)
    def _():
        st_ref[...] = jnp.zeros_like(st_ref)

    r = r_ref[...]
    v = v_ref[...]
    kk = kk_ref[...]
    lw = lw_ref[...]
    kd = kd_ref[...]
    bb = bb_ref[...]

    p1 = lw.astype(BF16)
    r1 = lw - p1.astype(F32)
    p2 = r1.astype(BF16)
    p3 = (r1 - p2.astype(F32)).astype(BF16)
    ts = ts_ref[...]
    acc = _dot(ts, p1) + _dot(ts, p2) + _dot(ts, p3)
    cl = acc[:TB]
    tot = acc[TB:]
    dec_in = jnp.exp(cl)
    dec_inv = jnp.exp(-cl)
    dec_ex = jnp.exp(cl - lw)
    dec_end = jnp.exp(tot - cl)
    dec_all = jnp.exp(tot)
    at = (-(kk * dec_ex)).astype(BF16)
    bt = (bb * dec_inv).astype(BF16)
    kt = (kd * dec_inv).astype(BF16)
    rt = (r * dec_in).astype(BF16)
    bh = (bb * dec_end).astype(BF16)
    kh = (kd * dec_end).astype(BF16)
    vb = v.astype(BF16)

    lane_head = lax.broadcasted_iota(jnp.int32, (CHUNK, QUAD), 1) // HEAD
    qi = lax.broadcasted_iota(jnp.int32, (QUAD, QUAD), 0)
    qj = lax.broadcasted_iota(jnp.int32, (QUAD, QUAD), 1)
    ti = qi & (CHUNK - 1)
    sj = qj & (CHUNK - 1)
    strict = (sj > ti) if reverse else (sj < ti)
    incl = (sj >= ti) if reverse else (sj <= ti)
    same_head = (qi // HEAD) == (qj // HEAD)

    def stack_heads(x):
        return jnp.concatenate([jnp.where(lane_head == h, x, jnp.zeros_like(x)) for h in range(nh)], 0)

    def fold_heads(x):
        out = x[:CHUNK]
        for h in range(1, nh):
            out = out + x[h * CHUNK:(h + 1) * CHUNK]
        return out

    units = [(c, q) for c in range(nchunk) for q in range(nquad)]
    pre = {}
    for c, q in units:
        rs = slice(c * CHUNK, (c + 1) * CHUNK)
        ls = slice(q * QUAD, (q + 1) * QUAD)
        a_st = stack_heads(at[rs, ls])
        r_st = stack_heads(rt[rs, ls])
        b_st = stack_heads(bt[rs, ls])
        k_st = stack_heads(kt[rs, ls])
        v_st = stack_heads(vb[rs, ls])
        qq = _dot_nt(jnp.concatenate([a_st, r_st], 0), jnp.concatenate([b_st, k_st], 0))
        n_ab = jnp.where(strict, qq[:QUAD, :QUAD], 0.0).astype(BF16)
        l_ak = jnp.where(strict, qq[:QUAD, QUAD:], 0.0).astype(BF16)
        m_r = jnp.concatenate([jnp.where(incl, qq[QUAD:, :QUAD], 0.0),
                               jnp.where(incl, qq[QUAD:, QUAD:], 0.0)], 1).astype(BF16)
        pre[(c, q)] = (a_st, v_st, n_ab, l_ak, m_r)
    inv = _unit_triangular_inverses([pre[u][2] for u in units])
    zq = jnp.zeros((CHUNK, QUAD), BF16)
    zf = jnp.zeros((CHUNK, QUAD), F32)
    rows = [slice(c * CHUNK, (c + 1) * CHUNK) for c, q in units]
    lanes = [slice(q * QUAD, (q + 1) * QUAD) for c, q in units]
    a_sts, v_sts, _, l_aks, m_rs = zip(*[pre[u] for u in units])
    lvs = [_dot(l_ak, v_st).astype(BF16) for l_ak, v_st in zip(l_aks, v_sts)]
    xws = [_dot(x, jnp.concatenate([a_st, lv], 1)) for x, a_st, lv in zip(inv, a_sts, lvs)]
    w_sts = [xw[:, :QUAD] for xw in xws]
    u0_sts = [xw[:, QUAD:] for xw in xws]
    rhats = [rt[rs, ls].astype(F32) + fold_heads(_dot(m_r[:, :QUAD], w_st.astype(BF16)))
             for rs, ls, m_r, w_st in zip(rows, lanes, m_rs, w_sts)]
    ycs = [fold_heads(_dot(m_r, jnp.concatenate([u0_st.astype(BF16), v_st], 0)))
           for m_r, u0_st, v_st in zip(m_rs, u0_sts, v_sts)]
    wuvs = [jnp.concatenate([fold_heads(w_st), fold_heads(u0_st), v[rs, ls], zf], 0).T.astype(BF16)
            for rs, ls, w_st, u0_st in zip(rows, lanes, w_sts, u0_sts)]
    ends = [jnp.concatenate([jnp.concatenate([bh[rs, ls], zq], 1), jnp.concatenate([zq, bh[rs, ls]], 1),
                             jnp.concatenate([zq, kh[rs, ls]], 1), jnp.concatenate([zq, zq], 1)], 0)
            for rs, ls in zip(rows, lanes)]
    ghs = [_dot(wuv, end) for wuv, end in zip(wuvs, ends)]
    fin = {}
    for u, rhat, yc, gh in zip(units, rhats, ycs, ghs):
        g = jnp.where(same_head, gh[:, :QUAD], 0.0).astype(BF16)
        hc = jnp.where(same_head, gh[:, QUAD:], 0.0)
        fin[u] = (rhat.astype(BF16), yc, g, hc)

    order = range(nchunk - 1, -1, -1) if reverse else range(nchunk)
    states = [st_ref[q] for q in range(nquad)]
    for c in order:
        rs = slice(c * CHUNK, (c + 1) * CHUNK)
        for q in range(nquad):
            ls = slice(q * QUAD, (q + 1) * QUAD)
            rhat, yc, g, hc = fin[(c, q)]
            st = states[q]
            stb = st.astype(BF16)
            y_ref[rs, ls] = _dot_nt(rhat, stb) + yc
            states[q] = st * dec_all[c * CHUNK:c * CHUNK + 1, ls] + _dot(stb, g) + hc
    for q in range(nquad):
        st_ref[q] = states[q]

    ones = ones_ref[...]
    inv_n = 1.0 / HEAD
    y = y_ref[...]
    mean = _split_dot(y, ones) * inv_n
    d = y - mean
    var = _split_dot(d * d, ones) * inv_n
    yn = d * lax.rsqrt(var + GN_EPS) * gng_ref[...] + gnb_ref[...]
    bonus = _split_dot(r * kd * rk_ref[...], ones) * v
    o_ref[...] = (yn + bonus) * g_ref[...]


def _scan(r, v, kk, lw, kd, bb, g, rk, gn_g, gn_b, ts, ones_bd, direction):
    b, tall, c = r.shape
    nt = tall // TB
    reverse = direction == 1
    if reverse:
        tmap = lambda t: jnp.where(t == 0, 0, nt - t)
    else:
        tmap = lambda t: t
    shared = pl.BlockSpec((None, TB, c), lambda i, t: (i, tmap(t), 0))
    perdir = pl.BlockSpec((None, None, TB, c), lambda i, t: (direction, i, tmap(t), 0))
    const2 = lambda i, t: (0, 0)
    return pl.pallas_call(
        functools.partial(_scan_kernel, reverse=reverse),
        grid=(b, nt),
        in_specs=[shared, shared, shared, perdir, perdir, perdir, perdir,
                  pl.BlockSpec((None, 1, c), lambda i, t: (direction, 0, 0)),
                  pl.BlockSpec((1, c), const2),
                  pl.BlockSpec((1, c), const2),
                  pl.BlockSpec(ts.shape, const2),
                  pl.BlockSpec((c, c), const2)],
        out_specs=pl.BlockSpec((None, TB, c), lambda i, t: (i, tmap(t), 0)),
        out_shape=jax.ShapeDtypeStruct((b, tall, c), F32),
        scratch_shapes=[pltpu.VMEM((c // QUAD, QUAD, QUAD), F32),
                        pltpu.VMEM((TB, c), F32)],
        compiler_params=_cp(("parallel", "arbitrary")),
        name="scan_rev" if reverse else "scan_fwd",
    )(r, v, kk, lw, kd, bb, g, rk, gn_g, gn_b, ts, ones_bd)


def _mm_kernel(a_ref, b_ref, o_ref, acc_ref):
    k = pl.program_id(2)

    @pl.when(k == 0)
    def _():
        acc_ref[...] = jnp.zeros_like(acc_ref)

    acc_ref[...] += _dot(a_ref[...], b_ref[...])

    @pl.when(k == pl.num_programs(2) - 1)
    def _():
        o_ref[...] = acc_ref[...]


def _matmul(a, b, tm, tn, tk):
    m, kd = a.shape
    n = b.shape[1]
    tm, tn, tk = min(tm, m), min(tn, n), min(tk, kd)
    return pl.pallas_call(
        _mm_kernel,
        grid=(m // tm, n // tn, kd // tk),
        in_specs=[pl.BlockSpec((tm, tk), lambda i, j, k: (i, k)),
                  pl.BlockSpec((tk, tn), lambda i, j, k: (k, j))],
        out_specs=pl.BlockSpec((tm, tn), lambda i, j, k: (i, j)),
        out_shape=jax.ShapeDtypeStruct((m, n), F32),
        scratch_shapes=[pltpu.VMEM((tm, tn), F32)],
        compiler_params=_cp(("parallel", "parallel", "arbitrary")),
        name="dft",
    )(a, b)


def _dft_matrix(seq):
    hi = seq // 128
    k = np.arange(seq, dtype=np.int64)[:, None]
    a_ang = 2.0 * np.pi * ((k * np.arange(hi)[None, :] * 128) % seq) / seq
    b_ang = 2.0 * np.pi * ((k * np.arange(128)[None, :]) % seq) / seq
    ca = jnp.asarray(np.cos(a_ang), F32)[:, :, None]
    sa = jnp.asarray(np.sin(a_ang), F32)[:, :, None]
    cb = jnp.asarray(np.cos(b_ang), F32)[:, None, :]
    sb = jnp.asarray(np.sin(b_ang), F32)[:, None, :]
    cos = (ca * cb - sa * sb).reshape(seq, seq)
    sin = (sa * cb + ca * sb).reshape(seq, seq)
    return jnp.concatenate([cos, sin], 1).astype(BF16)


def _out_kernel(x_ref, o0_ref, o1_ref, fn_ref, g0_ref, b0_ref, gt_ref, sc_ref, sh_ref, wt_ref, wb_ref, bf_ref,
                g1_ref, b1_ref, wr_ref, br_ref, tri_ref, one_ref,
                x1_o, h2_o, idx_o, gate_o, rank_o, cnt_o):
    xn = _ln(x_ref[...], g0_ref[...], b0_ref[...])
    rw = (o0_ref[...] + o1_ref[...]).astype(BF16)
    fn = (fn_ref[...] + bf_ref[...]).astype(BF16)
    mo = _dot(rw, wt_ref[...]) + _dot(fn, wb_ref[...])
    x1 = _ln(DEEPNORM_ALPHA * xn + gt_ref[...] * mo, g1_ref[...], b1_ref[...])
    h2 = x1 * (1.0 + sc_ref[...]) + sh_ref[...]
    x1_o[...] = x1
    _store_token_rows(h2_o, h2)

    vals = _dot_nt(wr_ref[...], h2, precision=HIGHEST) + br_ref[...]
    ne = vals.shape[0]
    rowid = lax.broadcasted_iota(jnp.int32, vals.shape, 0)
    sels, tops, idxs = [], [], []
    for _ in range(TOP_K):
        mx = jnp.max(vals, axis=0, keepdims=True)
        ix = jnp.min(jnp.where(vals == mx, rowid, ne), axis=0, keepdims=True)
        sel = rowid == ix
        vals = jnp.where(sel, -jnp.inf, vals)
        sels.append(sel)
        tops.append(mx)
        idxs.append(ix)
    ex = [jnp.exp(m - tops[0]) for m in tops]
    den = ex[0] + ex[1] + ex[2] + ex[3]
    gate_o[...] = jnp.concatenate([e / den for e in ex], 0)
    idx_o[...] = jnp.concatenate(idxs, 0)
    onehot = jnp.zeros(vals.shape, F32)
    for sel in sels:
        onehot = onehot + jnp.where(sel, 1.0, 0.0)
    ohb = onehot.astype(BF16)
    before = _dot(ohb, tri_ref[...])
    rank_o[...] = jnp.concatenate(
        [jnp.sum(jnp.where(sel, before, 0.0), axis=0, keepdims=True) for sel in sels], 0).astype(jnp.int32)
    cnt_o[...] = _dot(ohb, one_ref[...])


def _outproj(x, o0, o1, fno, ln0_g, ln0_b, gt1, sc2, sh2, w_top, w_bot, b_fno, ln1_g, ln1_b, w_rt, b_r, tri, ones_col):
    b, seq, d = x.shape
    c = o0.shape[-1]
    nt = seq // TB
    ntile = b * nt
    ne = w_rt.shape[0]
    const2 = lambda i, t: (0, 0)
    vec = pl.BlockSpec((1, d), const2)
    mod = pl.BlockSpec((None, 1, d), lambda i, t: (i, 0, 0))
    tok = lambda i, t: (0, i * nt + t)
    return pl.pallas_call(
        _out_kernel,
        grid=(b, nt),
        in_specs=[pl.BlockSpec((None, TB, d), lambda i, t: (i, t, 0)),
                  pl.BlockSpec((None, TB, c), lambda i, t: (i, t + 1, 0)),
                  pl.BlockSpec((None, TB, c), lambda i, t: (i, t + 1, 0)),
                  pl.BlockSpec((TB, c), lambda i, t: (t, i)),
                  vec, vec, mod, mod, mod,
                  pl.BlockSpec((c, d), const2),
                  pl.BlockSpec((c, d), const2),
                  pl.BlockSpec((1, c), const2),
                  vec, vec,
                  pl.BlockSpec((ne, d), const2),
                  pl.BlockSpec((ne, 1), const2),
                  pl.BlockSpec((TB, TB), const2),
                  pl.BlockSpec((TB, 128), const2)],
        out_specs=[pl.BlockSpec((None, TB, d), lambda i, t: (i, t, 0)),
                   pl.BlockSpec((TB * (d // LANES), LANES), lambda i, t: (i * nt + t, 0)),
                   pl.BlockSpec((TOP_K, TB), tok),
                   pl.BlockSpec((TOP_K, TB), tok),
                   pl.BlockSpec((TOP_K, TB), tok),
                   pl.BlockSpec((None, ne, 128), lambda i, t: (i * nt + t, 0, 0))],
        out_shape=[jax.ShapeDtypeStruct((b, seq, d), F32),
                   jax.ShapeDtypeStruct((b * seq * (d // LANES), LANES), F32),
                   jax.ShapeDtypeStruct((TOP_K, b * seq), jnp.int32),
                   jax.ShapeDtypeStruct((TOP_K, b * seq), F32),
                   jax.ShapeDtypeStruct((TOP_K, b * seq), jnp.int32),
                   jax.ShapeDtypeStruct((ntile, ne, 128), F32)],
        compiler_params=_cp(("parallel", "arbitrary")),
        name="outproj",
    )(x, o0, o1, fno, ln0_g, ln0_b, gt1, sc2, sh2, w_top, w_bot, b_fno, ln1_g, ln1_b, w_rt, b_r, tri, ones_col)


def _slot_kernel(idx_ref, rank_ref, base_ref, o_ref):
    idx = idx_ref[...]
    base = base_ref[...]
    rowid = lax.broadcasted_iota(jnp.int32, (base.shape[0], idx.shape[1]), 0)
    rows = [jnp.sum(jnp.where(rowid == idx[k:k + 1], base, 0), axis=0, keepdims=True) for k in range(TOP_K)]
    o_ref[...] = rank_ref[...] + jnp.concatenate(rows, 0)


def _slots(idx, rank, base):
    ntile, ne, _ = base.shape
    return pl.pallas_call(
        _slot_kernel,
        grid=(ntile,),
        in_specs=[pl.BlockSpec((TOP_K, TB), lambda i: (0, i)),
                  pl.BlockSpec((TOP_K, TB), lambda i: (0, i)),
                  pl.BlockSpec((None, ne, 1), lambda i: (i, 0, 0))],
        out_specs=pl.BlockSpec((None, TOP_K, TB), lambda i: (i, 0, 0)),
        out_shape=jax.ShapeDtypeStruct((ntile, TOP_K, TB), jnp.int32),
        compiler_params=_cp(("parallel",)),
        name="slot",
    )(idx, rank, base)


def _row_copy(src, src_row, dst, dst_row, sem, pieces):
    return pltpu.make_async_copy(src.at[pl.ds(pl.multiple_of(src_row * pieces, pieces), pieces)],
                                 dst.at[pl.ds(pl.multiple_of(dst_row * pieces, pieces), pieces)], sem)


def _dispatch_kernel(pad_ref, slot_hbm, h_ref, xs_ref, slot_smem, zrow_ref, sem, ssem):
    i = pl.program_id(0)
    pieces = zrow_ref.shape[0]
    cp = pltpu.make_async_copy(slot_hbm.at[i], slot_smem, ssem)
    cp.start()
    cp.wait()

    def issue(j, carry):
        for k in range(TOP_K):
            _row_copy(h_ref, j, xs_ref, slot_smem[k, j], sem, pieces).start()
        return carry

    lax.fori_loop(0, TB, issue, 0)

    def drain(j, carry):
        for k in range(TOP_K):
            _row_copy(h_ref, 0, xs_ref, 0, sem, pieces).wait()
        return carry

    lax.fori_loop(0, TB, drain, 0)

    @pl.when(i == pl.num_programs(0) - 1)
    def _():
        zrow_ref[...] = jnp.zeros_like(zrow_ref)

        def per_expert(e, carry):
            start = pad_ref[e]
            n = pad_ref[N_EXPERTS + e]

            def zi(q, c2):
                _row_copy(zrow_ref, 0, xs_ref, start + q, sem, pieces).start()
                return c2

            lax.fori_loop(0, n, zi, 0)

            def zw(q, c2):
                _row_copy(zrow_ref, 0, xs_ref, 0, sem, pieces).wait()
                return c2

            lax.fori_loop(0, n, zw, 0)
            return carry

        lax.fori_loop(0, N_EXPERTS, per_expert, 0)


def _dispatch(padinfo, slot, h2, n_slots, pieces):
    ntile = h2.shape[0] // (TB * pieces)
    return pl.pallas_call(
        _dispatch_kernel,
        grid_spec=pltpu.PrefetchScalarGridSpec(
            num_scalar_prefetch=1,
            grid=(ntile,),
            in_specs=[pl.BlockSpec(memory_space=pl.ANY),
                      pl.BlockSpec((TB * pieces, LANES), lambda i, pad: (i, 0))],
            out_specs=pl.BlockSpec(memory_space=pl.ANY),
            scratch_shapes=[pltpu.SMEM((TOP_K, TB), jnp.int32),
                            pltpu.VMEM((pieces, LANES), F32),
                            pltpu.SemaphoreType.DMA,
                            pltpu.SemaphoreType.DMA]),
        out_shape=jax.ShapeDtypeStruct((n_slots * pieces, LANES), F32),
        compiler_params=_cp(("arbitrary",)),
        name="dispatch",
    )(padinfo, slot, h2)


def _expert_kernel(be_ref, nu_ref, x_ref, w1_ref, b1_ref, w2_ref, b2_ref, perm_ref, o_ref, w1b_ref, w2b_ref):
    i = pl.program_id(0)
    prev = be_ref[jnp.maximum(i - 1, 0)]
    changed = jnp.logical_or(i == 0, be_ref[i] != prev)
    dff2 = w1_ref.shape[1]
    nblk = dff2 // 256

    @pl.when(changed)
    def _():
        perm = perm_ref[...]
        for j in range(nblk):
            cs = slice(j * 256, (j + 1) * 256)
            w1b_ref[:, cs] = _dot(w1_ref[:, cs].astype(BF16), perm).astype(BF16)
        w2b_ref[...] = w2_ref[...].astype(BF16)

    @pl.when(i < nu_ref[0])
    def _():
        x = _load_token_rows(x_ref, EBLK, w1_ref.shape[0] // LANES)
        u = _dot(x.astype(BF16), w1b_ref[...]) + b1_ref[...]
        acts = []
        for j in range(nblk):
            glu = jnp.minimum(u[:, j * 256:j * 256 + 128], SWIGLU_LIMIT)
            lin = jnp.clip(u[:, j * 256 + 128:(j + 1) * 256], -SWIGLU_LIMIT, SWIGLU_LIMIT)
            acts.append(glu * _sigmoid(SWIGLU_ALPHA * glu) * (lin + 1.0))
        act = jnp.concatenate(acts, 1).astype(BF16)
        _store_token_rows(o_ref, _dot(act, w2b_ref[...]) + b2_ref[...])

    @pl.when(i >= nu_ref[0])
    def _():
        o_ref[...] = jnp.zeros_like(o_ref)


def _experts(block_e, n_used, xs, w1, b1p, w2, b2, perm):
    ne, d, dff2 = w1.shape
    dff = w2.shape[1]
    pieces = d // LANES
    nblocks = xs.shape[0] // (EBLK * pieces)
    return pl.pallas_call(
        _expert_kernel,
        grid_spec=pltpu.PrefetchScalarGridSpec(
            num_scalar_prefetch=2,
            grid=(nblocks,),
            in_specs=[pl.BlockSpec((EBLK * pieces, LANES), lambda i, be, nu: (jnp.minimum(i, nu[0] - 1), 0)),
                      pl.BlockSpec((None, d, dff2), lambda i, be, nu: (be[i], 0, 0)),
                      pl.BlockSpec((None, 1, dff2), lambda i, be, nu: (be[i], 0, 0)),
                      pl.BlockSpec((None, dff, d), lambda i, be, nu: (be[i], 0, 0)),
                      pl.BlockSpec((None, 1, d), lambda i, be, nu: (be[i], 0, 0)),
                      pl.BlockSpec((256, 256), lambda i, be, nu: (0, 0))],
            out_specs=pl.BlockSpec((EBLK * pieces, LANES), lambda i, be, nu: (i, 0)),
            scratch_shapes=[pltpu.VMEM((d, dff2), BF16),
                            pltpu.VMEM((dff, d), BF16)]),
        out_shape=jax.ShapeDtypeStruct(xs.shape, F32),
        compiler_params=_cp(("arbitrary",)),
        name="experts",
    )(block_e, n_used, xs, w1, b1p, w2, b2, perm)


def _combine_kernel(slot_hbm, ys_ref, gate_ref, x1_ref, gt_ref, g_ref, b_ref, eye_ref, o_ref, slot_smem, buf_ref, sem, ssem):
    i = pl.program_id(0)
    pieces = o_ref.shape[1] // LANES
    cp = pltpu.make_async_copy(slot_hbm.at[i], slot_smem, ssem)
    cp.start()
    cp.wait()

    def issue(j, carry):
        for k in range(TOP_K):
            _row_copy(ys_ref, slot_smem[k, j], buf_ref.at[k], j, sem, pieces).start()
        return carry

    lax.fori_loop(0, TB, issue, 0)

    def drain(j, carry):
        for k in range(TOP_K):
            _row_copy(ys_ref, 0, buf_ref.at[k], 0, sem, pieces).wait()
        return carry

    lax.fori_loop(0, TB, drain, 0)

    gt = _dot_nt(eye_ref[...], gate_ref[...], precision=HIGHEST)
    f = gt[:, 0:1] * _load_token_rows(buf_ref.at[0], TB, pieces)
    for k in range(1, TOP_K):
        f = f + gt[:, k:k + 1] * _load_token_rows(buf_ref.at[k], TB, pieces)
    o_ref[...] = _ln(DEEPNORM_ALPHA * x1_ref[...] + gt_ref[...] * f, g_ref[...], b_ref[...])


def _combine(slot, ys, gates, x1, gt2, ln_g, ln_b, eye, tiles_per_batch):
    n, d = x1.shape
    ntile = n // TB
    return pl.pallas_call(
        _combine_kernel,
        grid=(ntile,),
        in_specs=[pl.BlockSpec(memory_space=pl.ANY),
                  pl.BlockSpec(memory_space=pl.ANY),
                  pl.BlockSpec((TOP_K, TB), lambda i: (0, i)),
                  pl.BlockSpec((TB, d), lambda i: (i, 0)),
                  pl.BlockSpec((None, 1, d), lambda i: (i // tiles_per_batch, 0, 0)),
                  pl.BlockSpec((1, d), lambda i: (0, 0)),
                  pl.BlockSpec((1, d), lambda i: (0, 0)),
                  pl.BlockSpec((TB, TB), lambda i: (0, 0))],
        out_specs=pl.BlockSpec((TB, d), lambda i: (i, 0)),
        out_shape=jax.ShapeDtypeStruct((n, d), F32),
        scratch_shapes=[pltpu.SMEM((TOP_K, TB), jnp.int32),
                        pltpu.VMEM((TOP_K, TB * (d // LANES), LANES), F32),
                        pltpu.SemaphoreType.DMA,
                        pltpu.SemaphoreType.DMA],
        compiler_params=_cp(("arbitrary",)),
        name="combine",
    )(slot, ys, gates, x1, gt2, ln_g, ln_b, eye)


def _scan_tables(reverse):
    t = np.arange(TB)
    same = (t[:, None] // CHUNK) == (t[None, :] // CHUNK)
    tri = (t[None, :] >= t[:, None]) if reverse else (t[None, :] <= t[:, None])
    return jnp.asarray(np.concatenate([same & tri, same], 0).astype(np.float32), BF16)


def _head_ones(c):
    h = np.arange(c) // HEAD
    return jnp.asarray((h[:, None] == h[None, :]).astype(np.float32), BF16)


def _deinterleave_perm():
    p = np.zeros((256, 256), np.float32)
    j = np.arange(128)
    p[2 * j, j] = 1.0
    p[2 * j + 1, 128 + j] = 1.0
    return jnp.asarray(p, BF16)


def kernel(x, c, ctx, c_ctx, ln0_g, ln0_b, w_ada, b_ada, w_in, mu_shift, w0, w2_decay, a0, a2_iclr, g2_gate, r_k, k_k, k_a, gn_g, gn_b, w_fno, b_fno, w_out, ln1_g, ln1_b, w_router, b_router, w1, b1, w2, b2, ln2_g, ln2_b):
    b, seq, d = x.shape
    ctx_len = ctx.shape[1]
    assert ctx_len == TB and seq % TB == 0 and w_ada.shape[0] == 1
    n_dir, cw = w0.shape[1], w0.shape[2]
    fgroups, gw = w_fno.shape[1], w_fno.shape[2]
    fw = fgroups * gw
    shift_w = mu_shift.shape[1]
    nd, na, ng = w2_decay.shape[2], a2_iclr.shape[2], g2_gate.shape[2]
    assert n_dir == 2 and shift_w == 3 * cw + 2 * (nd + na + ng)
    ne = w_router.shape[2]
    n_tok = b * seq
    row = lambda a: a.reshape(1, -1)

    rows = -(-(b + 1) // 8) * 8
    cc = jnp.zeros((rows, d), F32).at[:b].set(c).at[b].set(c_ctx)
    mod = _ada(cc, w_ada[0], b_ada[0])
    sh1, sc1, gt1, sh2, sc2, gt2 = [mod[:b, i * d:(i + 1) * d] for i in range(6)]
    sh1c, sc1c = mod[b, :d], mod[b, d:2 * d]
    scsel = jnp.stack([jnp.broadcast_to(sc1c, (b, d)), sc1], 1)[:, :, None, :]
    shsel = jnp.stack([jnp.broadcast_to(sh1c, (b, d)), sh1], 1)[:, :, None, :]
    mod3 = lambda a: a[:, None, :]

    cidx = np.arange(gw)
    ang = 2.0 * np.pi * ((cidx[:, None] * cidx[None, :]) % gw) / gw
    norm = 1.0 / math.sqrt(seq * gw)
    csc = jnp.asarray(np.stack([np.cos(ang) * norm, -np.sin(ang) * norm]), F32)
    win_f3 = w_in[0][:, shift_w:].reshape(d, fgroups, gw).transpose(1, 0, 2)
    wfc, wfs = _fold(csc, w_fno[0], win_f3)
    unf = lambda a: a.transpose(1, 0, 2).reshape(d, fw)
    w3 = jnp.concatenate([w_in[0][:, :shift_w], unf(wfc), unf(wfs)], 1).astype(BF16)

    s_all, fcat = _inproj(ctx, x, scsel, shsel, row(ln0_g), row(ln0_b), w3, shift_w, fw)

    def pad_dir(w, width):
        out = jnp.zeros((2, 2 * width, cw), F32)
        return out.at[0, :width].set(w[0]).at[1, width:].set(w[1]).astype(BF16)

    ones_bd = _head_ones(cw)
    r, v, kk, lw, kd, bb, g = _prep(
        s_all, row(mu_shift[0]), w0[0][:, None, :], pad_dir(w2_decay[0], nd), a0[0][:, None, :],
        pad_dir(a2_iclr[0], na), pad_dir(g2_gate[0], ng), row(k_k[0]), row(k_a[0]), ones_bd, cw)

    rk = r_k[0].reshape(2, 1, cw)
    outs = [_scan(r, v, kk, lw, kd, bb, g, rk, row(gn_g[0]), row(gn_b[0]), _scan_tables(dr == 1), ones_bd, dr)
            for dr in range(2)]

    fno = _matmul(_dft_matrix(seq), fcat.reshape(2 * seq, b * fw), 1024, 2048, 1024)

    ids = np.arange(TB)
    tri = jnp.asarray((ids[:, None] < ids[None, :]).astype(np.float32), BF16)
    ones_col = jnp.ones((TB, 128), BF16)
    wo = w_out[0].astype(BF16)
    x1, h2, idx, gates, rank, cnt = _outproj(
        x, outs[0], outs[1], fno, row(ln0_g), row(ln0_b), mod3(gt1), mod3(sc2), mod3(sh2), wo[:cw], wo[cw:],
        row(b_fno[0]), row(ln1_g[0]), row(ln1_b[0]), w_router[0].T, b_router[0].reshape(ne, 1), tri, ones_col)

    ntile = n_tok // TB
    cnt_t = cnt[:, :, 0].astype(jnp.int32)
    counts = jnp.sum(cnt_t, 0)
    padded = (counts + EBLK - 1) // EBLK * EBLK
    pends = jnp.cumsum(padded)
    pstarts = pends - padded
    base = (pstarts[None, :] + jnp.cumsum(cnt_t, 0) - cnt_t)[:, :, None]
    nblocks = (n_tok * TOP_K) // EBLK + ne
    n_slots = nblocks * EBLK
    block_start = jnp.arange(nblocks, dtype=jnp.int32) * EBLK
    block_e = jnp.minimum(jnp.sum((pends[None, :] <= block_start[:, None]).astype(jnp.int32), 1), ne - 1)
    n_used = (pends[-1:] // EBLK).astype(jnp.int32)
    padinfo = jnp.concatenate([pstarts + counts, padded - counts]).astype(jnp.int32)

    slot = _slots(idx, rank, base)
    xs = _dispatch(padinfo, slot, h2, n_slots, d // LANES)
    b1p = b1[0].reshape(ne, -1, 128, 2).transpose(0, 1, 3, 2).reshape(ne, 1, -1)
    ys = _experts(block_e, n_used, xs, w1[0], b1p, w2[0], b2[0][:, None, :], _deinterleave_perm())
    out = _combine(slot, ys, gates, x1.reshape(n_tok, d), mod3(gt2), row(ln2_g[0]), row(ln2_b[0]),
                   jnp.eye(TB, dtype=F32), seq // TB)
    return out.reshape(b, seq, d)
```

```python
import functools
import math

import numpy as np
import jax
import jax.numpy as jnp
from jax import lax
from jax.experimental import pallas as pl
from jax.experimental.pallas import tpu as pltpu

F32 = jnp.float32
BF16 = jnp.bfloat16
HIGHEST = lax.Precision.HIGHEST

GRID_W = 64
HEAD = 64
CHUNK = 64
TB = 256
N_EXPERTS = 32
TOP_K = 4
EBLK = 512
SWIGLU_LIMIT = 7.0
SWIGLU_ALPHA = 1.702
LN_EPS = 1e-5
GN_EPS = 64e-5
DEEPNORM_ALPHA = 2.0 ** 0.25
DECAY_SCALE = math.exp(-0.5)
LANES = 128
VMEM_LIMIT = 56 * 1024 * 1024


def _cp(sem, vmem=VMEM_LIMIT):
    return pltpu.CompilerParams(dimension_semantics=sem, vmem_limit_bytes=vmem)


def _sigmoid(x):
    return 1.0 / (1.0 + jnp.exp(-x))


def _ln(u, g, b):
    mean = jnp.mean(u, -1, keepdims=True)
    d = u - mean
    var = jnp.mean(d * d, -1, keepdims=True)
    return d * lax.rsqrt(var + LN_EPS) * g + b


def _dot(a, b):
    return jnp.dot(a, b, preferred_element_type=F32)


def _dot_nt(a, b, precision=None):
    return lax.dot_general(a, b, (((1,), (1,)), ((), ())), precision=precision, preferred_element_type=F32)


def _split_dot(x, w):
    hi = x.astype(BF16)
    lo = (x - hi.astype(F32)).astype(BF16)
    return _dot(hi, w) + _dot(lo, w)


def _load_token_rows(ref, n, pieces):
    return jnp.concatenate([ref[pl.ds(j, n, stride=pieces), :] for j in range(pieces)], 1)


def _store_token_rows(ref, val):
    n, width = val.shape
    pieces = width // LANES
    for j in range(pieces):
        ref[pl.ds(j, n, stride=pieces), :] = val[:, j * LANES:(j + 1) * LANES]


def _ada_kernel(c_ref, w_ref, b_ref, o_ref):
    c = c_ref[...]
    s = c * _sigmoid(c)
    o_ref[...] = jnp.dot(s, w_ref[...], precision=HIGHEST, preferred_element_type=F32) + b_ref[...]


def _ada(cc, w_ada, b_ada):
    rows, d = cc.shape
    n = w_ada.shape[1]
    tn = 1024
    return pl.pallas_call(
        _ada_kernel,
        grid=(n // tn,),
        in_specs=[pl.BlockSpec((rows, d), lambda j: (0, 0)),
                  pl.BlockSpec((d, tn), lambda j: (0, j)),
                  pl.BlockSpec((1, tn), lambda j: (0, j))],
        out_specs=pl.BlockSpec((rows, tn), lambda j: (0, j)),
        out_shape=jax.ShapeDtypeStruct((rows, n), F32),
        compiler_params=_cp(("parallel",)),
        name="ada",
    )(cc, w_ada, b_ada.reshape(1, n))


def _fold_kernel(cs_ref, wf_ref, win_ref, oc_ref, os_ref):
    wf = wf_ref[...]
    mc = jnp.dot(cs_ref[0], wf, precision=HIGHEST, preferred_element_type=F32)
    ms = jnp.dot(cs_ref[1], wf, precision=HIGHEST, preferred_element_type=F32)
    w = win_ref[...]
    oc_ref[...] = jnp.dot(w, mc, precision=HIGHEST, preferred_element_type=F32)
    os_ref[...] = jnp.dot(w, ms, precision=HIGHEST, preferred_element_type=F32)


def _fold(csc, w_fno, win_f3):
    g, d, gw = win_f3.shape
    return pl.pallas_call(
        _fold_kernel,
        grid=(g,),
        in_specs=[pl.BlockSpec((2, gw, gw), lambda i: (0, 0, 0)),
                  pl.BlockSpec((None, gw, gw), lambda i: (i, 0, 0)),
                  pl.BlockSpec((None, d, gw), lambda i: (i, 0, 0))],
        out_specs=[pl.BlockSpec((None, d, gw), lambda i: (i, 0, 0)),
                   pl.BlockSpec((None, d, gw), lambda i: (i, 0, 0))],
        out_shape=[jax.ShapeDtypeStruct((g, d, gw), F32)] * 2,
        compiler_params=_cp(("parallel",)),
        name="fold",
    )(csc, w_fno, win_f3)


def _in_kernel(ctx_ref, x_ref, sc_ref, sh_ref, g_ref, b_ref, w_ref, s_ref, f_ref, *, shift_w, fw):
    xin = jnp.where(pl.program_id(1) == 0, ctx_ref[...], x_ref[...])
    xn = _ln(xin, g_ref[...], b_ref[...])
    h = xn * (1.0 + sc_ref[...]) + sh_ref[...]
    p = _dot(h.astype(BF16), w_ref[...])
    s_ref[...] = p[:, :shift_w]
    f_ref[0] = p[:, shift_w:shift_w + fw].astype(BF16)
    f_ref[1] = p[:, shift_w + fw:].astype(BF16)


def _inproj(ctx, x, scsel, shsel, ln_g, ln_b, w3, shift_w, fw):
    b, seq, d = x.shape
    tall = ctx.shape[1] + seq
    nt = tall // TB
    wn = w3.shape[1]
    return pl.pallas_call(
        functools.partial(_in_kernel, shift_w=shift_w, fw=fw),
        grid=(b, nt),
        in_specs=[pl.BlockSpec((None, TB, d), lambda i, t: (i, 0, 0)),
                  pl.BlockSpec((None, TB, d), lambda i, t: (i, jnp.maximum(t - 1, 0), 0)),
                  pl.BlockSpec((None, None, 1, d), lambda i, t: (i, jnp.minimum(t, 1), 0, 0)),
                  pl.BlockSpec((None, None, 1, d), lambda i, t: (i, jnp.minimum(t, 1), 0, 0)),
                  pl.BlockSpec((1, d), lambda i, t: (0, 0)),
                  pl.BlockSpec((1, d), lambda i, t: (0, 0)),
                  pl.BlockSpec((d, wn), lambda i, t: (0, 0))],
        out_specs=[pl.BlockSpec((None, TB, shift_w), lambda i, t: (i, t, 0)),
                   pl.BlockSpec((2, TB, fw), lambda i, t: (0, jnp.maximum(t - 1, 0), i))],
        out_shape=[jax.ShapeDtypeStruct((b, tall, shift_w), F32),
                   jax.ShapeDtypeStruct((2, seq, b * fw), BF16)],
        compiler_params=_cp(("parallel", "arbitrary")),
        name="inproj",
    )(ctx, x, scsel, shsel, ln_g, ln_b, w3)


def _prep_kernel(s_ref, sp_ref, sn_ref, mu_ref, w0_ref, w2d_ref, a0_ref, a2_ref, g2_ref, kkw_ref, ka_ref, ones_ref,
                 r_o, v_o, kk_o, lw_o, kd_o, bb_o, g_o, *, c):
    t = pl.program_id(1)
    nt = pl.num_programs(1)
    s = s_ref[...]
    idx = lax.broadcasted_iota(jnp.int32, (TB, 1), 0)
    col = idx & (GRID_W - 1)
    is_ctx = t == 0
    lmask = jnp.where(is_ctx, idx, col) == 0
    rmask = jnp.where(is_ctx, idx - (TB - 1), col - (GRID_W - 1)) == 0
    left = jnp.where(lmask, 0.0, pltpu.roll(s, 1, 0))
    right = jnp.where(rmask, 0.0, pltpu.roll(s, TB - 1, 0))
    up = jnp.concatenate([jnp.where(t == 1, 0.0, sp_ref[...]), s[:TB - GRID_W]], 0)
    down = jnp.concatenate([s[GRID_W:], jnp.where(t == nt - 1, 0.0, sn_ref[...])], 0)
    ud = jnp.where(is_ctx, 0.0, up + down)
    sh = (ud + left + right) * jnp.where(is_ctx, 0.5, 0.25)
    m = s + mu_ref[...] * (sh - s)

    r = m[:, :c]
    k = m[:, c:2 * c]
    v = m[:, 2 * c:3 * c]
    o = 3 * c
    nd = w2d_ref.shape[1]
    na = a2_ref.shape[1]
    ng = g2_ref.shape[1]
    wd = jnp.tanh(m[:, o:o + nd]).astype(BF16)
    ad = m[:, o + nd:o + nd + na].astype(BF16)
    gd = _sigmoid(m[:, o + nd + na:o + nd + na + ng]).astype(BF16)
    kk = k * kkw_ref[...]
    ss = _split_dot(kk * kk, ones_ref[...])
    kk = kk / jnp.maximum(jnp.sqrt(ss), 1e-12)
    r_o[...] = r
    v_o[...] = v
    kk_o[...] = kk
    for d in range(2):
        wl = w0_ref[d] + _dot(wd, w2d_ref[d])
        lw_o[d] = -DECAY_SCALE * _sigmoid(wl)
        a = _sigmoid(a0_ref[d] + _dot(ad, a2_ref[d]))
        g_o[d] = _dot(gd, g2_ref[d])
        kd_o[d] = k * (1.0 + (a - 1.0) * ka_ref[...])
        bb_o[d] = kk * a


def _prep(s_all, mu, w0, w2d, a0, a2, g2, k_k, k_a, ones_bd, c):
    b, tall, sw = s_all.shape
    nt = tall // TB
    nhb = tall // GRID_W
    hb = TB // GRID_W
    const2 = lambda i, t: (0, 0)
    const3 = lambda i, t: (0, 0, 0)
    o1 = pl.BlockSpec((None, TB, c), lambda i, t: (i, t, 0))
    o2 = pl.BlockSpec((2, None, TB, c), lambda i, t: (0, i, t, 0))
    s1 = jax.ShapeDtypeStruct((b, tall, c), F32)
    s2 = jax.ShapeDtypeStruct((2, b, tall, c), F32)
    return pl.pallas_call(
        functools.partial(_prep_kernel, c=c),
        grid=(b, nt),
        in_specs=[pl.BlockSpec((None, TB, sw), lambda i, t: (i, t, 0)),
                  pl.BlockSpec((None, GRID_W, sw), lambda i, t: (i, jnp.maximum(t * hb - 1, 0), 0)),
                  pl.BlockSpec((None, GRID_W, sw), lambda i, t: (i, jnp.minimum(t * hb + hb, nhb - 1), 0)),
                  pl.BlockSpec((1, sw), const2),
                  pl.BlockSpec(w0.shape, const3),
                  pl.BlockSpec(w2d.shape, const3),
                  pl.BlockSpec(a0.shape, const3),
                  pl.BlockSpec(a2.shape, const3),
                  pl.BlockSpec(g2.shape, const3),
                  pl.BlockSpec((1, c), const2),
                  pl.BlockSpec((1, c), const2),
                  pl.BlockSpec((c, c), const2)],
        out_specs=[o1, o1, o1, o2, o2, o2, o2],
        out_shape=[s1, s1, s1, s2, s2, s2, s2],
        compiler_params=_cp(("parallel", "arbitrary")),
        name="prep",
    )(s_all, s_all, s_all, mu, w0, w2d, a0, a2, g2, k_k, k_a, ones_bd)


QUAD = 4 * HEAD


def _unit_triangular_inverses(ns):
    ti = lax.broadcasted_iota(jnp.int32, ns[0].shape, 0)
    tj = lax.broadcasted_iota(jnp.int32, ns[0].shape, 1)

    def same_block(shift):
        return (ti >> shift) == (tj >> shift)

    zero = jnp.zeros_like(ns[0])
    n8 = [jnp.where(same_block(3), n, zero) for n in ns]
    n8s = [_dot(a, a).astype(BF16) for a in n8]
    n8q = [_dot(a, a).astype(BF16) for a in n8s]
    xs = [jnp.where(ti == tj, jnp.ones_like(a), a) for a in n8]
    xs = [(x.astype(F32) + _dot(x, a)).astype(BF16) for x, a in zip(xs, n8s)]
    xs = [(x.astype(F32) + _dot(x, a)).astype(BF16) for x, a in zip(xs, n8q)]
    for shift in (3, 4, 5):
        offs = [jnp.where(same_block(shift + 1), jnp.where(same_block(shift), zero, n), zero) for n in ns]
        fs = [_dot(x, off).astype(BF16) for x, off in zip(xs, offs)]
        xs = [(x.astype(F32) + _dot(f, x)).astype(BF16) for x, f in zip(xs, fs)]
    return xs


def _scan_kernel(r_ref, v_ref, kk_ref, lw_ref, kd_ref, bb_ref, g_ref, rk_ref, gng_ref, gnb_ref, ts_ref, ones_ref,
                 o_ref, st_ref, y_ref, *, reverse):
    t = pl.program_id(1)
    width = r_ref.shape[-1]
    nquad = width // QUAD
    nchunk = TB // CHUNK
    nh = QUAD // HEAD

    @pl.when(t == 0)
    def _():
        st_ref[...] = jnp.zeros_like(st_ref)

    r = r_ref[...]
    v = v_ref[...]
    kk = kk_ref[...]
    lw = lw_ref[...]
    kd = kd_ref[...]
    bb = bb_ref[...]

    p1 = lw.astype(BF16)
    r1 = lw - p1.astype(F32)
    p2 = r1.astype(BF16)
    p3 = (r1 - p2.astype(F32)).astype(BF16)
    ts = ts_ref[...]
    acc = _dot(ts, p1) + _dot(ts, p2) + _dot(ts, p3)
    cl = acc[:TB]
    tot = acc[TB:]
    dec_in = jnp.exp(cl)
    dec_inv = jnp.exp(-cl)
    dec_ex = jnp.exp(cl - lw)
    dec_end = jnp.exp(tot - cl)
    dec_all = jnp.exp(tot)
    at = (-(kk * dec_ex)).astype(BF16)
    bt = (bb * dec_inv).astype(BF16)
    kt = (kd * dec_inv).astype(BF16)
    rt = (r * dec_in).astype(BF16)
    bh = (bb * dec_end).astype(BF16)
    kh = (kd * dec_end).astype(BF16)
    vb = v.astype(BF16)

    lane_head = lax.broadcasted_iota(jnp.int32, (CHUNK, QUAD), 1) // HEAD
    qi = lax.broadcasted_iota(jnp.int32, (QUAD, QUAD), 0)
    qj = lax.broadcasted_iota(jnp.int32, (QUAD, QUAD), 1)
    ti = qi & (CHUNK - 1)
    sj = qj & (CHUNK - 1)
    strict = (sj > ti) if reverse else (sj < ti)
    incl = (sj >= ti) if reverse else (sj <= ti)
    same_head = (qi // HEAD) == (qj // HEAD)

    def stack_heads(x):
        return jnp.concatenate([jnp.where(lane_head == h, x, jnp.zeros_like(x)) for h in range(nh)], 0)

    def fold_heads(x):
        out = x[:CHUNK]
        for h in range(1, nh):
            out = out + x[h * CHUNK:(h + 1) * CHUNK]
        return out

    units = [(c, q) for c in range(nchunk) for q in range(nquad)]
    pre = {}
    for c, q in units:
        rs = slice(c * CHUNK, (c + 1) * CHUNK)
        ls = slice(q * QUAD, (q + 1) * QUAD)
        a_st = stack_heads(at[rs, ls])
        r_st = stack_heads(rt[rs, ls])
        b_st = stack_heads(bt[rs, ls])
        k_st = stack_heads(kt[rs, ls])
        v_st = stack_heads(vb[rs, ls])
        qq = _dot_nt(jnp.concatenate([a_st, r_st], 0), jnp.concatenate([b_st, k_st], 0))
        n_ab = jnp.where(strict, qq[:QUAD, :QUAD], 0.0).astype(BF16)
        l_ak = jnp.where(strict, qq[:QUAD, QUAD:], 0.0).astype(BF16)
        m_r = jnp.concatenate([jnp.where(incl, qq[QUAD:, :QUAD], 0.0),
                               jnp.where(incl, qq[QUAD:, QUAD:], 0.0)], 1).astype(BF16)
        pre[(c, q)] = (a_st, v_st, n_ab, l_ak, m_r)
    inv = _unit_triangular_inverses([pre[u][2] for u in units])
    zq = jnp.zeros((CHUNK, QUAD), BF16)
    zf = jnp.zeros((CHUNK, QUAD), F32)
    rows = [slice(c * CHUNK, (c + 1) * CHUNK) for c, q in units]
    lanes = [slice(q * QUAD, (q + 1) * QUAD) for c, q in units]
    a_sts, v_sts, _, l_aks, m_rs = zip(*[pre[u] for u in units])
    lvs = [_dot(l_ak, v_st).astype(BF16) for l_ak, v_st in zip(l_aks, v_sts)]
    xws = [_dot(x, jnp.concatenate([a_st, lv], 1)) for x, a_st, lv in zip(inv, a_sts, lvs)]
    w_sts = [xw[:, :QUAD] for xw in xws]
    u0_sts = [xw[:, QUAD:] for xw in xws]
    rhats = [rt[rs, ls].astype(F32) + fold_heads(_dot(m_r[:, :QUAD], w_st.astype(BF16)))
             for rs, ls, m_r, w_st in zip(rows, lanes, m_rs, w_sts)]
    ycs = [fold_heads(_dot(m_r, jnp.concatenate([u0_st.astype(BF16), v_st], 0)))
           for m_r, u0_st, v_st in zip(m_rs, u0_sts, v_sts)]
    wuvs = [jnp.concatenate([fold_heads(w_st), fold_heads(u0_st), v[rs, ls], zf], 0).T.astype(BF16)
            for rs, ls, w_st, u0_st in zip(rows, lanes, w_sts, u0_sts)]
    ends = [jnp.concatenate([jnp.concatenate([bh[rs, ls], zq], 1), jnp.concatenate([zq, bh[rs, ls]], 1),
                             jnp.concatenate([zq, kh[rs, ls]], 1), jnp.concatenate([zq, zq], 1)], 0)
            for rs, ls in zip(rows, lanes)]
    ghs = [_dot(wuv, end) for wuv, end in zip(wuvs, ends)]
    fin = {}
    for u, rhat, yc, gh in zip(units, rhats, ycs, ghs):
        g = jnp.where(same_head, gh[:, :QUAD], 0.0).astype(BF16)
        hc = jnp.where(same_head, gh[:, QUAD:], 0.0)
        fin[u] = (rhat.astype(BF16), yc, g, hc)

    order = range(nchunk - 1, -1, -1) if reverse else range(nchunk)
    states = [st_ref[q] for q in range(nquad)]
    for c in order:
        rs = slice(c * CHUNK, (c + 1) * CHUNK)
        for q in range(nquad):
            ls = slice(q * QUAD, (q + 1) * QUAD)
            rhat, yc, g, hc = fin[(c, q)]
            st = states[q]
            stb = st.astype(BF16)
            y_ref[rs, ls] = _dot_nt(rhat, stb) + yc
            states[q] = st * dec_all[c * CHUNK:c * CHUNK + 1, ls] + _dot(stb, g) + hc
    for q in range(nquad):
        st_ref[q] = states[q]

    ones = ones_ref[...]
    inv_n = 1.0 / HEAD
    y = y_ref[...]
    mean = _split_dot(y, ones) * inv_n
    d = y - mean
    var = _split_dot(d * d, ones) * inv_n
    yn = d * lax.rsqrt(var + GN_EPS) * gng_ref[...] + gnb_ref[...]
    bonus = _split_dot(r * kd * rk_ref[...], ones) * v
    o_ref[...] = (yn + bonus) * g_ref[...]


def _scan(r, v, kk, lw, kd, bb, g, rk, gn_g, gn_b, ts, ones_bd, direction):
    b, tall, c = r.shape
    nt = tall // TB
    reverse = direction == 1
    if reverse:
        tmap = lambda t: jnp.where(t == 0, 0, nt - t)
    else:
        tmap = lambda t: t
    shared = pl.BlockSpec((None, TB, c), lambda i, t: (i, tmap(t), 0))
    perdir = pl.BlockSpec((None, None, TB, c), lambda i, t: (direction, i, tmap(t), 0))
    const2 = lambda i, t: (0, 0)
    return pl.pallas_call(
        functools.partial(_scan_kernel, reverse=reverse),
        grid=(b, nt),
        in_specs=[shared, shared, shared, perdir, perdir, perdir, perdir,
                  pl.BlockSpec((None, 1, c), lambda i, t: (direction, 0, 0)),
                  pl.BlockSpec((1, c), const2),
                  pl.BlockSpec((1, c), const2),
                  pl.BlockSpec(ts.shape, const2),
                  pl.BlockSpec((c, c), const2)],
        out_specs=pl.BlockSpec((None, TB, c), lambda i, t: (i, tmap(t), 0)),
        out_shape=jax.ShapeDtypeStruct((b, tall, c), F32),
        scratch_shapes=[pltpu.VMEM((c // QUAD, QUAD, QUAD), F32),
                        pltpu.VMEM((TB, c), F32)],
        compiler_params=_cp(("parallel", "arbitrary")),
        name="scan_rev" if reverse else "scan_fwd",
    )(r, v, kk, lw, kd, bb, g, rk, gn_g, gn_b, ts, ones_bd)


def _mm_kernel(a_ref, b_ref, o_ref, acc_ref):
    k = pl.program_id(2)

    @pl.when(k == 0)
    def _():
        acc_ref[...] = jnp.zeros_like(acc_ref)

    acc_ref[...] += _dot(a_ref[...], b_ref[...])

    @pl.when(k == pl.num_programs(2) - 1)
    def _():
        o_ref[...] = acc_ref[...]


def _matmul(a, b, tm, tn, tk):
    m, kd = a.shape
    n = b.shape[1]
    tm, tn, tk = min(tm, m), min(tn, n), min(tk, kd)
    return pl.pallas_call(
        _mm_kernel,
        grid=(m // tm, n // tn, kd // tk),
        in_specs=[pl.BlockSpec((tm, tk), lambda i, j, k: (i, k)),
                  pl.BlockSpec((tk, tn), lambda i, j, k: (k, j))],
        out_specs=pl.BlockSpec((tm, tn), lambda i, j, k: (i, j)),
        out_shape=jax.ShapeDtypeStruct((m, n), F32),
        scratch_shapes=[pltpu.VMEM((tm, tn), F32)],
        compiler_params=_cp(("parallel", "parallel", "arbitrary")),
        name="dft",
    )(a, b)


def _dft_matrix(seq):
    hi = seq // 128
    k = np.arange(seq, dtype=np.int64)[:, None]
    a_ang = 2.0 * np.pi * ((k * np.arange(hi)[None, :] * 128) % seq) / seq
    b_ang = 2.0 * np.pi * ((k * np.arange(128)[None, :]) % seq) / seq
    ca = jnp.asarray(np.cos(a_ang), F32)[:, :, None]
    sa = jnp.asarray(np.sin(a_ang), F32)[:, :, None]
    cb = jnp.asarray(np.cos(b_ang), F32)[:, None, :]
    sb = jnp.asarray(np.sin(b_ang), F32)[:, None, :]
    cos = (ca * cb - sa * sb).reshape(seq, seq)
    sin = (sa * cb + ca * sb).reshape(seq, seq)
    return jnp.concatenate([cos, sin], 1).astype(BF16)


def _out_kernel(x_ref, o0_ref, o1_ref, fn_ref, g0_ref, b0_ref, gt_ref, sc_ref, sh_ref, wt_ref, wb_ref, bf_ref,
                g1_ref, b1_ref, wr_ref, br_ref, tri_ref, one_ref,
                x1_o, h2_o, idx_o, gate_o, rank_o, cnt_o):
    xn = _ln(x_ref[...], g0_ref[...], b0_ref[...])
    rw = (o0_ref[...] + o1_ref[...]).astype(BF16)
    fn = (fn_ref[...] + bf_ref[...]).astype(BF16)
    mo = _dot(rw, wt_ref[...]) + _dot(fn, wb_ref[...])
    x1 = _ln(DEEPNORM_ALPHA * xn + gt_ref[...] * mo, g1_ref[...], b1_ref[...])
    h2 = x1 * (1.0 + sc_ref[...]) + sh_ref[...]
    x1_o[...] = x1
    _store_token_rows(h2_o, h2)

    vals = _dot_nt(wr_ref[...], h2, precision=HIGHEST) + br_ref[...]
    ne = vals.shape[0]
    rowid = lax.broadcasted_iota(jnp.int32, vals.shape, 0)
    sels, tops, idxs = [], [], []
    for _ in range(TOP_K):
        mx = jnp.max(vals, axis=0, keepdims=True)
        ix = jnp.min(jnp.where(vals == mx, rowid, ne), axis=0, keepdims=True)
        sel = rowid == ix
        vals = jnp.where(sel, -jnp.inf, vals)
        sels.append(sel)
        tops.append(mx)
        idxs.append(ix)
    ex = [jnp.exp(m - tops[0]) for m in tops]
    den = ex[0] + ex[1] + ex[2] + ex[3]
    gate_o[...] = jnp.concatenate([e / den for e in ex], 0)
    idx_o[...] = jnp.concatenate(idxs, 0)
    onehot = jnp.zeros(vals.shape, F32)
    for sel in sels:
        onehot = onehot + jnp.where(sel, 1.0, 0.0)
    ohb = onehot.astype(BF16)
    before = _dot(ohb, tri_ref[...])
    rank_o[...] = jnp.concatenate(
        [jnp.sum(jnp.where(sel, before, 0.0), axis=0, keepdims=True) for sel in sels], 0).astype(jnp.int32)
    cnt_o[...] = _dot(ohb, one_ref[...])


def _outproj(x, o0, o1, fno, ln0_g, ln0_b, gt1, sc2, sh2, w_top, w_bot, b_fno, ln1_g, ln1_b, w_rt, b_r, tri, ones_col):
    b, seq, d = x.shape
    c = o0.shape[-1]
    nt = seq // TB
    ntile = b * nt
    ne = w_rt.shape[0]
    const2 = lambda i, t: (0, 0)
    vec = pl.BlockSpec((1, d), const2)
    mod = pl.BlockSpec((None, 1, d), lambda i, t: (i, 0, 0))
    tok = lambda i, t: (0, i * nt + t)
    return pl.pallas_call(
        _out_kernel,
        grid=(b, nt),
        in_specs=[pl.BlockSpec((None, TB, d), lambda i, t: (i, t, 0)),
                  pl.BlockSpec((None, TB, c), lambda i, t: (i, t + 1, 0)),
                  pl.BlockSpec((None, TB, c), lambda i, t: (i, t + 1, 0)),
                  pl.BlockSpec((TB, c), lambda i, t: (t, i)),
                  vec, vec, mod, mod, mod,
                  pl.BlockSpec((c, d), const2),
                  pl.BlockSpec((c, d), const2),
                  pl.BlockSpec((1, c), const2),
                  vec, vec,
                  pl.BlockSpec((ne, d), const2),
                  pl.BlockSpec((ne, 1), const2),
                  pl.BlockSpec((TB, TB), const2),
                  pl.BlockSpec((TB, 128), const2)],
        out_specs=[pl.BlockSpec((None, TB, d), lambda i, t: (i, t, 0)),
                   pl.BlockSpec((TB * (d // LANES), LANES), lambda i, t: (i * nt + t, 0)),
                   pl.BlockSpec((TOP_K, TB), tok),
                   pl.BlockSpec((TOP_K, TB), tok),
                   pl.BlockSpec((TOP_K, TB), tok),
                   pl.BlockSpec((None, ne, 128), lambda i, t: (i * nt + t, 0, 0))],
        out_shape=[jax.ShapeDtypeStruct((b, seq, d), F32),
                   jax.ShapeDtypeStruct((b * seq * (d // LANES), LANES), F32),
                   jax.ShapeDtypeStruct((TOP_K, b * seq), jnp.int32),
                   jax.ShapeDtypeStruct((TOP_K, b * seq), F32),
                   jax.ShapeDtypeStruct((TOP_K, b * seq), jnp.int32),
                   jax.ShapeDtypeStruct((ntile, ne, 128), F32)],
        compiler_params=_cp(("parallel", "arbitrary")),
        name="outproj",
    )(x, o0, o1, fno, ln0_g, ln0_b, gt1, sc2, sh2, w_top, w_bot, b_fno, ln1_g, ln1_b, w_rt, b_r, tri, ones_col)


def _slot_kernel(idx_ref, rank_ref, base_ref, o_ref):
    idx = idx_ref[...]
    base = base_ref[...]
    rowid = lax.broadcasted_iota(jnp.int32, (base.shape[0], idx.shape[1]), 0)
    rows = [jnp.sum(jnp.where(rowid == idx[k:k + 1], base, 0), axis=0, keepdims=True) for k in range(TOP_K)]
    o_ref[...] = rank_ref[...] + jnp.concatenate(rows, 0)


def _slots(idx, rank, base):
    ntile, ne, _ = base.shape
    return pl.pallas_call(
        _slot_kernel,
        grid=(ntile,),
        in_specs=[pl.BlockSpec((TOP_K, TB), lambda i: (0, i)),
                  pl.BlockSpec((TOP_K, TB), lambda i: (0, i)),
                  pl.BlockSpec((None, ne, 1), lambda i: (i, 0, 0))],
        out_specs=pl.BlockSpec((None, TOP_K, TB), lambda i: (i, 0, 0)),
        out_shape=jax.ShapeDtypeStruct((ntile, TOP_K, TB), jnp.int32),
        compiler_params=_cp(("parallel",)),
        name="slot",
    )(idx, rank, base)


def _row_copy(src, src_row, dst, dst_row, sem, pieces):
    return pltpu.make_async_copy(src.at[pl.ds(pl.multiple_of(src_row * pieces, pieces), pieces)],
                                 dst.at[pl.ds(pl.multiple_of(dst_row * pieces, pieces), pieces)], sem)


def _dispatch_kernel(pad_ref, slot_hbm, h_ref, xs_ref, slot_smem, zrow_ref, sem, ssem):
    i = pl.program_id(0)
    pieces = zrow_ref.shape[0]
    cp = pltpu.make_async_copy(slot_hbm.at[i], slot_smem, ssem)
    cp.start()
    cp.wait()

    def issue(j, carry):
        for k in range(TOP_K):
            _row_copy(h_ref, j, xs_ref, slot_smem[k, j], sem, pieces).start()
        return carry

    lax.fori_loop(0, TB, issue, 0)

    def drain(j, carry):
        for k in range(TOP_K):
            _row_copy(h_ref, 0, xs_ref, 0, sem, pieces).wait()
        return carry

    lax.fori_loop(0, TB, drain, 0)

    @pl.when(i == pl.num_programs(0) - 1)
    def _():
        zrow_ref[...] = jnp.zeros_like(zrow_ref)

        def per_expert(e, carry):
            start = pad_ref[e]
            n = pad_ref[N_EXPERTS + e]

            def zi(q, c2):
                _row_copy(zrow_ref, 0, xs_ref, start + q, sem, pieces).start()
                return c2

            lax.fori_loop(0, n, zi, 0)

            def zw(q, c2):
                _row_copy(zrow_ref, 0, xs_ref, 0, sem, pieces).wait()
                return c2

            lax.fori_loop(0, n, zw, 0)
            return carry

        lax.fori_loop(0, N_EXPERTS, per_expert, 0)


def _dispatch(padinfo, slot, h2, n_slots, pieces):
    ntile = h2.shape[0] // (TB * pieces)
    return pl.pallas_call(
        _dispatch_kernel,
        grid_spec=pltpu.PrefetchScalarGridSpec(
            num_scalar_prefetch=1,
            grid=(ntile,),
            in_specs=[pl.BlockSpec(memory_space=pl.ANY),
                      pl.BlockSpec((TB * pieces, LANES), lambda i, pad: (i, 0))],
            out_specs=pl.BlockSpec(memory_space=pl.ANY),
            scratch_shapes=[pltpu.SMEM((TOP_K, TB), jnp.int32),
                            pltpu.VMEM((pieces, LANES), F32),
                            pltpu.SemaphoreType.DMA,
                            pltpu.SemaphoreType.DMA]),
        out_shape=jax.ShapeDtypeStruct((n_slots * pieces, LANES), F32),
        compiler_params=_cp(("arbitrary",)),
        name="dispatch",
    )(padinfo, slot, h2)


def _expert_kernel(be_ref, nu_ref, x_ref, w1_ref, b1_ref, w2_ref, b2_ref, perm_ref, o_ref, w1b_ref, w2b_ref):
    i = pl.program_id(0)
    prev = be_ref[jnp.maximum(i - 1, 0)]
    changed = jnp.logical_or(i == 0, be_ref[i] != prev)
    dff2 = w1_ref.shape[1]
    nblk = dff2 // 256

    @pl.when(changed)
    def _():
        perm = perm_ref[...]
        for j in range(nblk):
            cs = slice(j * 256, (j + 1) * 256)
            w1b_ref[:, cs] = _dot(w1_ref[:, cs].astype(BF16), perm).astype(BF16)
        w2b_ref[...] = w2_ref[...].astype(BF16)

    @pl.when(i < nu_ref[0])
    def _():
        x = _load_token_rows(x_ref, EBLK, w1_ref.shape[0] // LANES)
        u = _dot(x.astype(BF16), w1b_ref[...]) + b1_ref[...]
        acts = []
        for j in range(nblk):
            glu = jnp.minimum(u[:, j * 256:j * 256 + 128], SWIGLU_LIMIT)
            lin = jnp.clip(u[:, j * 256 + 128:(j + 1) * 256], -SWIGLU_LIMIT, SWIGLU_LIMIT)
            acts.append(glu * _sigmoid(SWIGLU_ALPHA * glu) * (lin + 1.0))
        act = jnp.concatenate(acts, 1).astype(BF16)
        _store_token_rows(o_ref, _dot(act, w2b_ref[...]) + b2_ref[...])

    @pl.when(i >= nu_ref[0])
    def _():
        o_ref[...] = jnp.zeros_like(o_ref)


def _experts(block_e, n_used, xs, w1, b1p, w2, b2, perm):
    ne, d, dff2 = w1.shape
    dff = w2.shape[1]
    pieces = d // LANES
    nblocks = xs.shape[0] // (EBLK * pieces)
    return pl.pallas_call(
        _expert_kernel,
        grid_spec=pltpu.PrefetchScalarGridSpec(
            num_scalar_prefetch=2,
            grid=(nblocks,),
            in_specs=[pl.BlockSpec((EBLK * pieces, LANES), lambda i, be, nu: (jnp.minimum(i, nu[0] - 1), 0)),
                      pl.BlockSpec((None, d, dff2), lambda i, be, nu: (be[i], 0, 0)),
                      pl.BlockSpec((None, 1, dff2), lambda i, be, nu: (be[i], 0, 0)),
                      pl.BlockSpec((None, dff, d), lambda i, be, nu: (be[i], 0, 0)),
                      pl.BlockSpec((None, 1, d), lambda i, be, nu: (be[i], 0, 0)),
                      pl.BlockSpec((256, 256), lambda i, be, nu: (0, 0))],
            out_specs=pl.BlockSpec((EBLK * pieces, LANES), lambda i, be, nu: (i, 0)),
            scratch_shapes=[pltpu.VMEM((d, dff2), BF16),
                            pltpu.VMEM((dff, d), BF16)]),
        out_shape=jax.ShapeDtypeStruct(xs.shape, F32),
        compiler_params=_cp(("arbitrary",)),
        name="experts",
    )(block_e, n_used, xs, w1, b1p, w2, b2, perm)


def _combine_kernel(slot_hbm, ys_ref, gate_ref, x1_ref, gt_ref, g_ref, b_ref, eye_ref, o_ref, slot_smem, buf_ref, sem, ssem):
    i = pl.program_id(0)
    pieces = o_ref.shape[1] // LANES
    cp = pltpu.make_async_copy(slot_hbm.at[i], slot_smem, ssem)
    cp.start()
    cp.wait()

    def issue(j, carry):
        for k in range(TOP_K):
            _row_copy(ys_ref, slot_smem[k, j], buf_ref.at[k], j, sem, pieces).start()
        return carry

    lax.fori_loop(0, TB, issue, 0)

    def drain(j, carry):
        for k in range(TOP_K):
            _row_copy(ys_ref, 0, buf_ref.at[k], 0, sem, pieces).wait()
        return carry

    lax.fori_loop(0, TB, drain, 0)

    gt = _dot_nt(eye_ref[...], gate_ref[...], precision=HIGHEST)
    f = gt[:, 0:1] * _load_token_rows(buf_ref.at[0], TB, pieces)
    for k in range(1, TOP_K):
        f = f + gt[:, k:k + 1] * _load_token_rows(buf_ref.at[k], TB, pieces)
    o_ref[...] = _ln(DEEPNORM_ALPHA * x1_ref[...] + gt_ref[...] * f, g_ref[...], b_ref[...])


def _combine(slot, ys, gates, x1, gt2, ln_g, ln_b, eye, tiles_per_batch):
    n, d = x1.shape
    ntile = n // TB
    return pl.pallas_call(
        _combine_kernel,
        grid=(ntile,),
        in_specs=[pl.BlockSpec(memory_space=pl.ANY),
                  pl.BlockSpec(memory_space=pl.ANY),
                  pl.BlockSpec((TOP_K, TB), lambda i: (0, i)),
                  pl.BlockSpec((TB, d), lambda i: (i, 0)),
                  pl.BlockSpec((None, 1, d), lambda i: (i // tiles_per_batch, 0, 0)),
                  pl.BlockSpec((1, d), lambda i: (0, 0)),
                  pl.BlockSpec((1, d), lambda i: (0, 0)),
                  pl.BlockSpec((TB, TB), lambda i: (0, 0))],
        out_specs=pl.BlockSpec((TB, d), lambda i: (i, 0)),
        out_shape=jax.ShapeDtypeStruct((n, d), F32),
        scratch_shapes=[pltpu.SMEM((TOP_K, TB), jnp.int32),
                        pltpu.VMEM((TOP_K, TB * (d // LANES), LANES), F32),
                        pltpu.SemaphoreType.DMA,
                        pltpu.SemaphoreType.DMA],
        compiler_params=_cp(("arbitrary",)),
        name="combine",
    )(slot, ys, gates, x1, gt2, ln_g, ln_b, eye)


def _scan_tables(reverse):
    t = np.arange(TB)
    same = (t[:, None] // CHUNK) == (t[None, :] // CHUNK)
    tri = (t[None, :] >= t[:, None]) if reverse else (t[None, :] <= t[:, None])
    return jnp.asarray(np.concatenate([same & tri, same], 0).astype(np.float32), BF16)


def _head_ones(c):
    h = np.arange(c) // HEAD
    return jnp.asarray((h[:, None] == h[None, :]).astype(np.float32), BF16)


def _deinterleave_perm():
    p = np.zeros((256, 256), np.float32)
    j = np.arange(128)
    p[2 * j, j] = 1.0
    p[2 * j + 1, 128 + j] = 1.0
    return jnp.asarray(p, BF16)


def kernel(x, c, ctx, c_ctx, ln0_g, ln0_b, w_ada, b_ada, w_in, mu_shift, w0, w2_decay, a0, a2_iclr, g2_gate, r_k, k_k, k_a, gn_g, gn_b, w_fno, b_fno, w_out, ln1_g, ln1_b, w_router, b_router, w1, b1, w2, b2, ln2_g, ln2_b):
    b, seq, d = x.shape
    ctx_len = ctx.shape[1]
    assert ctx_len == TB and seq % TB == 0 and w_ada.shape[0] == 1
    n_dir, cw = w0.shape[1], w0.shape[2]
    fgroups, gw = w_fno.shape[1], w_fno.shape[2]
    fw = fgroups * gw
    shift_w = mu_shift.shape[1]
    nd, na, ng = w2_decay.shape[2], a2_iclr.shape[2], g2_gate.shape[2]
    assert n_dir == 2 and shift_w == 3 * cw + 2 * (nd + na + ng)
    ne = w_router.shape[2]
    n_tok = b * seq
    row = lambda a: a.reshape(1, -1)

    rows = -(-(b + 1) // 8) * 8
    cc = jnp.zeros((rows, d), F32).at[:b].set(c).at[b].set(c_ctx)
    mod = _ada(cc, w_ada[0], b_ada[0])
    sh1, sc1, gt1, sh2, sc2, gt2 = [mod[:b, i * d:(i + 1) * d] for i in range(6)]
    sh1c, sc1c = mod[b, :d], mod[b, d:2 * d]
    scsel = jnp.stack([jnp.broadcast_to(sc1c, (b, d)), sc1], 1)[:, :, None, :]
    shsel = jnp.stack([jnp.broadcast_to(sh1c, (b, d)), sh1], 1)[:, :, None, :]
    mod3 = lambda a: a[:, None, :]

    cidx = np.arange(gw)
    ang = 2.0 * np.pi * ((cidx[:, None] * cidx[None, :]) % gw) / gw
    norm = 1.0 / math.sqrt(seq * gw)
    csc = jnp.asarray(np.stack([np.cos(ang) * norm, -np.sin(ang) * norm]), F32)
    win_f3 = w_in[0][:, shift_w:].reshape(d, fgroups, gw).transpose(1, 0, 2)
    wfc, wfs = _fold(csc, w_fno[0], win_f3)
    unf = lambda a: a.transpose(1, 0, 2).reshape(d, fw)
    w3 = jnp.concatenate([w_in[0][:, :shift_w], unf(wfc), unf(wfs)], 1).astype(BF16)

    s_all, fcat = _inproj(ctx, x, scsel, shsel, row(ln0_g), row(ln0_b), w3, shift_w, fw)

    def pad_dir(w, width):
        out = jnp.zeros((2, 2 * width, cw), F32)
        return out.at[0, :width].set(w[0]).at[1, width:].set(w[1]).astype(BF16)

    ones_bd = _head_ones(cw)
    r, v, kk, lw, kd, bb, g = _prep(
        s_all, row(mu_shift[0]), w0[0][:, None, :], pad_dir(w2_decay[0], nd), a0[0][:, None, :],
        pad_dir(a2_iclr[0], na), pad_dir(g2_gate[0], ng), row(k_k[0]), row(k_a[0]), ones_bd, cw)

    rk = r_k[0].reshape(2, 1, cw)
    outs = [_scan(r, v, kk, lw, kd, bb, g, rk, row(gn_g[0]), row(gn_b[0]), _scan_tables(dr == 1), ones_bd, dr)
            for dr in range(2)]

    fno = _matmul(_dft_matrix(seq), fcat.reshape(2 * seq, b * fw), 1024, 2048, 1024)

    ids = np.arange(TB)
    tri = jnp.asarray((ids[:, None] < ids[None, :]).astype(np.float32), BF16)
    ones_col = jnp.ones((TB, 128), BF16)
    wo = w_out[0].astype(BF16)
    x1, h2, idx, gates, rank, cnt = _outproj(
        x, outs[0], outs[1], fno, row(ln0_g), row(ln0_b), mod3(gt1), mod3(sc2), mod3(sh2), wo[:cw], wo[cw:],
        row(b_fno[0]), row(ln1_g[0]), row(ln1_b[0]), w_router[0].T, b_router[0].reshape(ne, 1), tri, ones_col)

    ntile = n_tok // TB
    cnt_t = cnt[:, :, 0].astype(jnp.int32)
    counts = jnp.sum(cnt_t, 0)
    padded = (counts + EBLK - 1) // EBLK * EBLK
    pends = jnp.cumsum(padded)
    pstarts = pends - padded
    base = (pstarts[None, :] + jnp.cumsum(cnt_t, 0) - cnt_t)[:, :, None]
    nblocks = (n_tok * TOP_K) // EBLK + ne
    n_slots = nblocks * EBLK
    block_start = jnp.arange(nblocks, dtype=jnp.int32) * EBLK
    block_e = jnp.minimum(jnp.sum((pends[None, :] <= block_start[:, None]).astype(jnp.int32), 1), ne - 1)
    n_used = (pends[-1:] // EBLK).astype(jnp.int32)
    padinfo = jnp.concatenate([pstarts + counts, padded - counts]).astype(jnp.int32)

    slot = _slots(idx, rank, base)
    xs = _dispatch(padinfo, slot, h2, n_slots, d // LANES)
    b1p = b1[0].reshape(ne, -1, 128, 2).transpose(0, 1, 3, 2).reshape(ne, 1, -1)
    ys = _experts(block_e, n_used, xs, w1[0], b1p, w2[0], b2[0][:, None, :], _deinterleave_perm())
    out = _combine(slot, ys, gates, x1.reshape(n_tok, d), mod3(gt2), row(ln2_g[0]), row(ln2_b[0]),
                   jnp.eye(TB, dtype=F32), seq // TB)
    return out.reshape(b, seq, d)
```

```python
import functools
import math

import numpy as np
import jax
import jax.numpy as jnp
from jax import lax
from jax.experimental import pallas as pl
from jax.experimental.pallas import tpu as pltpu

F32 = jnp.float32
BF16 = jnp.bfloat16
HIGHEST = lax.Precision.HIGHEST

GRID_W = 64
HEAD = 64
CHUNK = 64
TB = 256
N_EXPERTS = 32
TOP_K = 4
EBLK = 512
SWIGLU_LIMIT = 7.0
SWIGLU_ALPHA = 1.702
LN_EPS = 1e-5
GN_EPS = 64e-5
DEEPNORM_ALPHA = 2.0 ** 0.25
DECAY_SCALE = math.exp(-0.5)
LANES = 128
VMEM_LIMIT = 56 * 1024 * 1024


def _cp(sem, vmem=VMEM_LIMIT):
    return pltpu.CompilerParams(dimension_semantics=sem, vmem_limit_bytes=vmem)


def _sigmoid(x):
    return 1.0 / (1.0 + jnp.exp(-x))


def _ln(u, g, b):
    mean = jnp.mean(u, -1, keepdims=True)
    d = u - mean
    var = jnp.mean(d * d, -1, keepdims=True)
    return d * lax.rsqrt(var + LN_EPS) * g + b


def _dot(a, b):
    return jnp.dot(a, b, preferred_element_type=F32)


def _dot_nt(a, b, precision=None):
    return lax.dot_general(a, b, (((1,), (1,)), ((), ())), precision=precision, preferred_element_type=F32)


def _split_dot(x, w):
    hi = x.astype(BF16)
    lo = (x - hi.astype(F32)).astype(BF16)
    return _dot(hi, w) + _dot(lo, w)


def _load_token_rows(ref, n, pieces):
    return jnp.concatenate([ref[pl.ds(j, n, stride=pieces), :] for j in range(pieces)], 1)


def _store_token_rows(ref, val):
    n, width = val.shape
    pieces = width // LANES
    for j in range(pieces):
        ref[pl.ds(j, n, stride=pieces), :] = val[:, j * LANES:(j + 1) * LANES]


def _ada_kernel(c_ref, w_ref, b_ref, o_ref):
    c = c_ref[...]
    s = c * _sigmoid(c)
    o_ref[...] = jnp.dot(s, w_ref[...], precision=HIGHEST, preferred_element_type=F32) + b_ref[...]


def _ada(cc, w_ada, b_ada):
    rows, d = cc.shape
    n = w_ada.shape[1]
    tn = 1024
    return pl.pallas_call(
        _ada_kernel,
        grid=(n // tn,),
        in_specs=[pl.BlockSpec((rows, d), lambda j: (0, 0)),
                  pl.BlockSpec((d, tn), lambda j: (0, j)),
                  pl.BlockSpec((1, tn), lambda j: (0, j))],
        out_specs=pl.BlockSpec((rows, tn), lambda j: (0, j)),
        out_shape=jax.ShapeDtypeStruct((rows, n), F32),
        compiler_params=_cp(("parallel",)),
        name="ada",
    )(cc, w_ada, b_ada.reshape(1, n))


def _fold_kernel(cs_ref, wf_ref, win_ref, oc_ref, os_ref):
    wf = wf_ref[...]
    mc = jnp.dot(cs_ref[0], wf, precision=HIGHEST, preferred_element_type=F32)
    ms = jnp.dot(cs_ref[1], wf, precision=HIGHEST, preferred_element_type=F32)
    w = win_ref[...]
    oc_ref[...] = jnp.dot(w, mc, precision=HIGHEST, preferred_element_type=F32)
    os_ref[...] = jnp.dot(w, ms, precision=HIGHEST, preferred_element_type=F32)


def _fold(csc, w_fno, win_f3):
    g, d, gw = win_f3.shape
    return pl.pallas_call(
        _fold_kernel,
        grid=(g,),
        in_specs=[pl.BlockSpec((2, gw, gw), lambda i: (0, 0, 0)),
                  pl.BlockSpec((None, gw, gw), lambda i: (i, 0, 0)),
                  pl.BlockSpec((None, d, gw), lambda i: (i, 0, 0))],
        out_specs=[pl.BlockSpec((None, d, gw), lambda i: (i, 0, 0)),
                   pl.BlockSpec((None, d, gw), lambda i: (i, 0, 0))],
        out_shape=[jax.ShapeDtypeStruct((g, d, gw), F32)] * 2,
        compiler_params=_cp(("parallel",)),
        name="fold",
    )(csc, w_fno, win_f3)


def _in_kernel(ctx_ref, x_ref, sc_ref, sh_ref, g_ref, b_ref, w_ref, s_ref, f_ref, *, shift_w, fw):
    xin = jnp.where(pl.program_id(1) == 0, ctx_ref[...], x_ref[...])
    xn = _ln(xin, g_ref[...], b_ref[...])
    h = xn * (1.0 + sc_ref[...]) + sh_ref[...]
    p = _dot(h.astype(BF16), w_ref[...])
    s_ref[...] = p[:, :shift_w]
    f_ref[0] = p[:, shift_w:shift_w + fw].astype(BF16)
    f_ref[1] = p[:, shift_w + fw:].astype(BF16)


def _inproj(ctx, x, scsel, shsel, ln_g, ln_b, w3, shift_w, fw):
    b, seq, d = x.shape
    tall = ctx.shape[1] + seq
    nt = tall // TB
    wn = w3.shape[1]
    return pl.pallas_call(
        functools.partial(_in_kernel, shift_w=shift_w, fw=fw),
        grid=(b, nt),
        in_specs=[pl.BlockSpec((None, TB, d), lambda i, t: (i, 0, 0)),
                  pl.BlockSpec((None, TB, d), lambda i, t: (i, jnp.maximum(t - 1, 0), 0)),
                  pl.BlockSpec((None, None, 1, d), lambda i, t: (i, jnp.minimum(t, 1), 0, 0)),
                  pl.BlockSpec((None, None, 1, d), lambda i, t: (i, jnp.minimum(t, 1), 0, 0)),
                  pl.BlockSpec((1, d), lambda i, t: (0, 0)),
                  pl.BlockSpec((1, d), lambda i, t: (0, 0)),
                  pl.BlockSpec((d, wn), lambda i, t: (0, 0))],
        out_specs=[pl.BlockSpec((None, TB, shift_w), lambda i, t: (i, t, 0)),
                   pl.BlockSpec((2, TB, fw), lambda i, t: (0, jnp.maximum(t - 1, 0), i))],
        out_shape=[jax.ShapeDtypeStruct((b, tall, shift_w), F32),
                   jax.ShapeDtypeStruct((2, seq, b * fw), BF16)],
        compiler_params=_cp(("parallel", "arbitrary")),
        name="inproj",
    )(ctx, x, scsel, shsel, ln_g, ln_b, w3)


def _prep_kernel(s_ref, sp_ref, sn_ref, mu_ref, w0_ref, w2d_ref, a0_ref, a2_ref, g2_ref, kkw_ref, ka_ref, ones_ref,
                 r_o, v_o, kk_o, lw_o, kd_o, bb_o, g_o, *, c):
    t = pl.program_id(1)
    nt = pl.num_programs(1)
    s = s_ref[...]
    idx = lax.broadcasted_iota(jnp.int32, (TB, 1), 0)
    col = idx & (GRID_W - 1)
    is_ctx = t == 0
    lmask = jnp.where(is_ctx, idx, col) == 0
    rmask = jnp.where(is_ctx, idx - (TB - 1), col - (GRID_W - 1)) == 0
    left = jnp.where(lmask, 0.0, pltpu.roll(s, 1, 0))
    right = jnp.where(rmask, 0.0, pltpu.roll(s, TB - 1, 0))
    up = jnp.concatenate([jnp.where(t == 1, 0.0, sp_ref[...]), s[:TB - GRID_W]], 0)
    down = jnp.concatenate([s[GRID_W:], jnp.where(t == nt - 1, 0.0, sn_ref[...])], 0)
    ud = jnp.where(is_ctx, 0.0, up + down)
    sh = (ud + left + right) * jnp.where(is_ctx, 0.5, 0.25)
    m = s + mu_ref[...] * (sh - s)

    r = m[:, :c]
    k = m[:, c:2 * c]
    v = m[:, 2 * c:3 * c]
    o = 3 * c
    nd = w2d_ref.shape[1]
    na = a2_ref.shape[1]
    ng = g2_ref.shape[1]
    wd = jnp.tanh(m[:, o:o + nd]).astype(BF16)
    ad = m[:, o + nd:o + nd + na].astype(BF16)
    gd = _sigmoid(m[:, o + nd + na:o + nd + na + ng]).astype(BF16)
    kk = k * kkw_ref[...]
    ss = _split_dot(kk * kk, ones_ref[...])
    kk = kk / jnp.maximum(jnp.sqrt(ss), 1e-12)
    r_o[...] = r
    v_o[...] = v
    kk_o[...] = kk
    for d in range(2):
        wl = w0_ref[d] + _dot(wd, w2d_ref[d])
        lw_o[d] = -DECAY_SCALE * _sigmoid(wl)
        a = _sigmoid(a0_ref[d] + _dot(ad, a2_ref[d]))
        g_o[d] = _dot(gd, g2_ref[d])
        kd_o[d] = k * (1.0 + (a - 1.0) * ka_ref[...])
        bb_o[d] = kk * a


def _prep(s_all, mu, w0, w2d, a0, a2, g2, k_k, k_a, ones_bd, c):
    b, tall, sw = s_all.shape
    nt = tall // TB
    nhb = tall // GRID_W
    hb = TB // GRID_W
    const2 = lambda i, t: (0, 0)
    const3 = lambda i, t: (0, 0, 0)
    o1 = pl.BlockSpec((None, TB, c), lambda i, t: (i, t, 0))
    o2 = pl.BlockSpec((2, None, TB, c), lambda i, t: (0, i, t, 0))
    s1 = jax.ShapeDtypeStruct((b, tall, c), F32)
    s2 = jax.ShapeDtypeStruct((2, b, tall, c), F32)
    return pl.pallas_call(
        functools.partial(_prep_kernel, c=c),
        grid=(b, nt),
        in_specs=[pl.BlockSpec((None, TB, sw), lambda i, t: (i, t, 0)),
                  pl.BlockSpec((None, GRID_W, sw), lambda i, t: (i, jnp.maximum(t * hb - 1, 0), 0)),
                  pl.BlockSpec((None, GRID_W, sw), lambda i, t: (i, jnp.minimum(t * hb + hb, nhb - 1), 0)),
                  pl.BlockSpec((1, sw), const2),
                  pl.BlockSpec(w0.shape, const3),
                  pl.BlockSpec(w2d.shape, const3),
                  pl.BlockSpec(a0.shape, const3),
                  pl.BlockSpec(a2.shape, const3),
                  pl.BlockSpec(g2.shape, const3),
                  pl.BlockSpec((1, c), const2),
                  pl.BlockSpec((1, c), const2),
                  pl.BlockSpec((c, c), const2)],
        out_specs=[o1, o1, o1, o2, o2, o2, o2],
        out_shape=[s1, s1, s1, s2, s2, s2, s2],
        compiler_params=_cp(("parallel", "arbitrary")),
        name="prep",
    )(s_all, s_all, s_all, mu, w0, w2d, a0, a2, g2, k_k, k_a, ones_bd)


QUAD = 4 * HEAD


def _unit_triangular_inverses(ns):
    ti = lax.broadcasted_iota(jnp.int32, ns[0].shape, 0)
    tj = lax.broadcasted_iota(jnp.int32, ns[0].shape, 1)

    def same_block(shift):
        return (ti >> shift) == (tj >> shift)

    zero = jnp.zeros_like(ns[0])
    n8 = [jnp.where(same_block(3), n, zero) for n in ns]
    n8s = [_dot(a, a).astype(BF16) for a in n8]
    n8q = [_dot(a, a).astype(BF16) for a in n8s]
    xs = [jnp.where(ti == tj, jnp.ones_like(a), a) for a in n8]
    xs = [(x.astype(F32) + _dot(x, a)).astype(BF16) for x, a in zip(xs, n8s)]
    xs = [(x.astype(F32) + _dot(x, a)).astype(BF16) for x, a in zip(xs, n8q)]
    for shift in (3, 4, 5):
        offs = [jnp.where(same_block(shift + 1), jnp.where(same_block(shift), zero, n), zero) for n in ns]
        fs = [_dot(x, off).astype(BF16) for x, off in zip(xs, offs)]
        xs = [(x.astype(F32) + _dot(f, x)).astype(BF16) for x, f in zip(xs, fs)]
    return xs


def _scan_kernel(r_ref, v_ref, kk_ref, lw_ref, kd_ref, bb_ref, g_ref, rk_ref, gng_ref, gnb_ref, ts_ref, ones_ref,
                 o_ref, st_ref, y_ref, *, reverse):
    t = pl.program_id(1)
    width = r_ref.shape[-1]
    nquad = width // QUAD
    nchunk = TB // CHUNK
    nh = QUAD // HEAD

    @pl.when(t == 0)
    def _():
        st_ref[...] = jnp.zeros_like(st_ref)

    r = r_ref[...]
    v = v_ref[...]
    kk = kk_ref[...]
    lw = lw_ref[...]
    kd = kd_ref[...]
    bb = bb_ref[...]

    p1 = lw.astype(BF16)
    r1 = lw - p1.astype(F32)
    p2 = r1.astype(BF16)
    p3 = (r1 - p2.astype(F32)).astype(BF16)
    ts = ts_ref[...]
    acc = _dot(ts, p1) + _dot(ts, p2) + _dot(ts, p3)
    cl = acc[:TB]
    tot = acc[TB:]
    dec_in = jnp.exp(cl)
    dec_inv = jnp.exp(-cl)
    dec_ex = jnp.exp(cl - lw)
    dec_end = jnp.exp(tot - cl)
    dec_all = jnp.exp(tot)
    at = (-(kk * dec_ex)).astype(BF16)
    bt = (bb * dec_inv).astype(BF16)
    kt = (kd * dec_inv).astype(BF16)
    rt = (r * dec_in).astype(BF16)
    bh = (bb * dec_end).astype(BF16)
    kh = (kd * dec_end).astype(BF16)
    vb = v.astype(BF16)

    lane_head = lax.broadcasted_iota(jnp.int32, (CHUNK, QUAD), 1) // HEAD
    qi = lax.broadcasted_iota(jnp.int32, (QUAD, QUAD), 0)
    qj = lax.broadcasted_iota(jnp.int32, (QUAD, QUAD), 1)
    ti = qi & (CHUNK - 1)
    sj = qj & (CHUNK - 1)
    strict = (sj > ti) if reverse else (sj < ti)
    incl = (sj >= ti) if reverse else (sj <= ti)
    same_head = (qi // HEAD) == (qj // HEAD)

    def stack_heads(x):
        return jnp.concatenate([jnp.where(lane_head == h, x, jnp.zeros_like(x)) for h in range(nh)], 0)

    def fold_heads(x):
        out = x[:CHUNK]
        for h in range(1, nh):
            out = out + x[h * CHUNK:(h + 1) * CHUNK]
        return out

    units = [(c, q) for c in range(nchunk) for q in range(nquad)]
    pre = {}
    for c, q in units:
        rs = slice(c * CHUNK, (c + 1) * CHUNK)
        ls = slice(q * QUAD, (q + 1) * QUAD)
        a_st = stack_heads(at[rs, ls])
        r_st = stack_heads(rt[rs, ls])
        b_st = stack_heads(bt[rs, ls])
        k_st = stack_heads(kt[rs, ls])
        v_st = stack_heads(vb[rs, ls])
        qq = _dot_nt(jnp.concatenate([a_st, r_st], 0), jnp.concatenate([b_st, k_st], 0))
        n_ab = jnp.where(strict, qq[:QUAD, :QUAD], 0.0).astype(BF16)
        l_ak = jnp.where(strict, qq[:QUAD, QUAD:], 0.0).astype(BF16)
        m_r = jnp.concatenate([jnp.where(incl, qq[QUAD:, :QUAD], 0.0),
                               jnp.where(incl, qq[QUAD:, QUAD:], 0.0)], 1).astype(BF16)
        pre[(c, q)] = (a_st, v_st, n_ab, l_ak, m_r)
    inv = _unit_triangular_inverses([pre[u][2] for u in units])
    zq = jnp.zeros((CHUNK, QUAD), BF16)
    zf = jnp.zeros((CHUNK, QUAD), F32)
    rows = [slice(c * CHUNK, (c + 1) * CHUNK) for c, q in units]
    lanes = [slice(q * QUAD, (q + 1) * QUAD) for c, q in units]
    a_sts, v_sts, _, l_aks, m_rs = zip(*[pre[u] for u in units])
    lvs = [_dot(l_ak, v_st).astype(BF16) for l_ak, v_st in zip(l_aks, v_sts)]
    xws = [_dot(x, jnp.concatenate([a_st, lv], 1)) for x, a_st, lv in zip(inv, a_sts, lvs)]
    w_sts = [xw[:, :QUAD] for xw in xws]
    u0_sts = [xw[:, QUAD:] for xw in xws]
    rhats = [rt[rs, ls].astype(F32) + fold_heads(_dot(m_r[:, :QUAD], w_st.astype(BF16)))
             for rs, ls, m_r, w_st in zip(rows, lanes, m_rs, w_sts)]
    ycs = [fold_heads(_dot(m_r, jnp.concatenate([u0_st.astype(BF16), v_st], 0)))
           for m_r, u0_st, v_st in zip(m_rs, u0_sts, v_sts)]
    wuvs = [jnp.concatenate([fold_heads(w_st), fold_heads(u0_st), v[rs, ls], zf], 0).T.astype(BF16)
            for rs, ls, w_st, u0_st in zip(rows, lanes, w_sts, u0_sts)]
    ends = [jnp.concatenate([jnp.concatenate([bh[rs, ls], zq], 1), jnp.concatenate([zq, bh[rs, ls]], 1),
                             jnp.concatenate([zq, kh[rs, ls]], 1), jnp.concatenate([zq, zq], 1)], 0)
            for rs, ls in zip(rows, lanes)]
    ghs = [_dot(wuv, end) for wuv, end in zip(wuvs, ends)]
    fin = {}
    for u, rhat, yc, gh in zip(units, rhats, ycs, ghs):
        g = jnp.where(same_head, gh[:, :QUAD], 0.0).astype(BF16)
        hc = jnp.where(same_head, gh[:, QUAD:], 0.0)
        fin[u] = (rhat.astype(BF16), yc, g, hc)

    order = range(nchunk - 1, -1, -1) if reverse else range(nchunk)
    states = [st_ref[q] for q in range(nquad)]
    for c in order:
        rs = slice(c * CHUNK, (c + 1) * CHUNK)
        for q in range(nquad):
            ls = slice(q * QUAD, (q + 1) * QUAD)
            rhat, yc, g, hc = fin[(c, q)]
            st = states[q]
            stb = st.astype(BF16)
            y_ref[rs, ls] = _dot_nt(rhat, stb) + yc
            states[q] = st * dec_all[c * CHUNK:c * CHUNK + 1, ls] + _dot(stb, g) + hc
    for q in range(nquad):
        st_ref[q] = states[q]

    ones = ones_ref[...]
    inv_n = 1.0 / HEAD
    y = y_ref[...]
    mean = _split_dot(y, ones) * inv_n
    d = y - mean
    var = _split_dot(d * d, ones) * inv_n
    yn = d * lax.rsqrt(var + GN_EPS) * gng_ref[...] + gnb_ref[...]
    bonus = _split_dot(r * kd * rk_ref[...], ones) * v
    o_ref[...] = (yn + bonus) * g_ref[...]


def _scan(r, v, kk, lw, kd, bb, g, rk, gn_g, gn_b, ts, ones_bd, direction):
    b, tall, c = r.shape
    nt = tall // TB
    reverse = direction == 1
    if reverse:
        tmap = lambda t: jnp.where(t == 0, 0, nt - t)
    else:
        tmap = lambda t: t
    shared = pl.BlockSpec((None, TB, c), lambda i, t: (i, tmap(t), 0))
    perdir = pl.BlockSpec((None, None, TB, c), lambda i, t: (direction, i, tmap(t), 0))
    const2 = lambda i, t: (0, 0)
    return pl.pallas_call(
        functools.partial(_scan_kernel, reverse=reverse),
        grid=(b, nt),
        in_specs=[shared, shared, shared, perdir, perdir, perdir, perdir,
                  pl.BlockSpec((None, 1, c), lambda i, t: (direction, 0, 0)),
                  pl.BlockSpec((1, c), const2),
                  pl.BlockSpec((1, c), const2),
                  pl.BlockSpec(ts.shape, const2),
                  pl.BlockSpec((c, c), const2)],
        out_specs=pl.BlockSpec((None, TB, c), lambda i, t: (i, tmap(t), 0)),
        out_shape=jax.ShapeDtypeStruct((b, tall, c), F32),
        scratch_shapes=[pltpu.VMEM((c // QUAD, QUAD, QUAD), F32),
                        pltpu.VMEM((TB, c), F32)],
        compiler_params=_cp(("parallel", "arbitrary")),
        name="scan_rev" if reverse else "scan_fwd",
    )(r, v, kk, lw, kd, bb, g, rk, gn_g, gn_b, ts, ones_bd)


def _mm_kernel(a_ref, b_ref, o_ref, acc_ref):
    k = pl.program_id(2)

    @pl.when(k == 0)
    def _():
        acc_ref[...] = jnp.zeros_like(acc_ref)

    acc_ref[...] += _dot(a_ref[...], b_ref[...])

    @pl.when(k == pl.num_programs(2) - 1)
    def _():
        o_ref[...] = acc_ref[...]


def _matmul(a, b, tm, tn, tk):
    m, kd = a.shape
    n = b.shape[1]
    tm, tn, tk = min(tm, m), min(tn, n), min(tk, kd)
    return pl.pallas_call(
        _mm_kernel,
        grid=(m // tm, n // tn, kd // tk),
        in_specs=[pl.BlockSpec((tm, tk), lambda i, j, k: (i, k)),
                  pl.BlockSpec((tk, tn), lambda i, j, k: (k, j))],
        out_specs=pl.BlockSpec((tm, tn), lambda i, j, k: (i, j)),
        out_shape=jax.ShapeDtypeStruct((m, n), F32),
        scratch_shapes=[pltpu.VMEM((tm, tn), F32)],
        compiler_params=_cp(("parallel", "parallel", "arbitrary")),
        name="dft",
    )(a, b)


def _dft_matrix(seq):
    hi = seq // 128
    k = np.arange(seq, dtype=np.int64)[:, None]
    a_ang = 2.0 * np.pi * ((k * np.arange(hi)[None, :] * 128) % seq) / seq
    b_ang = 2.0 * np.pi * ((k * np.arange(128)[None, :]) % seq) / seq
    ca = jnp.asarray(np.cos(a_ang), F32)[:, :, None]
    sa = jnp.asarray(np.sin(a_ang), F32)[:, :, None]
    cb = jnp.asarray(np.cos(b_ang), F32)[:, None, :]
    sb = jnp.asarray(np.sin(b_ang), F32)[:, None, :]
    cos = (ca * cb - sa * sb).reshape(seq, seq)
    sin = (sa * cb + ca * sb).reshape(seq, seq)
    return jnp.concatenate([cos, sin], 1).astype(BF16)


def _out_kernel(x_ref, o0_ref, o1_ref, fn_ref, g0_ref, b0_ref, gt_ref, sc_ref, sh_ref, wt_ref, wb_ref, bf_ref,
                g1_ref, b1_ref, wr_ref, br_ref, tri_ref, one_ref,
                x1_o, h2_o, idx_o, gate_o, rank_o, cnt_o):
    xn = _ln(x_ref[...], g0_ref[...], b0_ref[...])
    rw = (o0_ref[...] + o1_ref[...]).astype(BF16)
    fn = (fn_ref[...] + bf_ref[...]).astype(BF16)
    mo = _dot(rw, wt_ref[...]) + _dot(fn, wb_ref[...])
    x1 = _ln(DEEPNORM_ALPHA * xn + gt_ref[...] * mo, g1_ref[...], b1_ref[...])
    h2 = x1 * (1.0 + sc_ref[...]) + sh_ref[...]
    x1_o[...] = x1
    _store_token_rows(h2_o, h2)

    vals = _dot_nt(wr_ref[...], h2, precision=HIGHEST) + br_ref[...]
    ne = vals.shape[0]
    rowid = lax.broadcasted_iota(jnp.int32, vals.shape, 0)
    sels, tops, idxs = [], [], []
    for _ in range(TOP_K):
        mx = jnp.max(vals, axis=0, keepdims=True)
        ix = jnp.min(jnp.where(vals == mx, rowid, ne), axis=0, keepdims=True)
        sel = rowid == ix
        vals = jnp.where(sel, -jnp.inf, vals)
        sels.append(sel)
        tops.append(mx)
        idxs.append(ix)
    ex = [jnp.exp(m - tops[0]) for m in tops]
    den = ex[0] + ex[1] + ex[2] + ex[3]
    gate_o[...] = jnp.concatenate([e / den for e in ex], 0)
    idx_o[...] = jnp.concatenate(idxs, 0)
    onehot = jnp.zeros(vals.shape, F32)
    for sel in sels:
        onehot = onehot + jnp.where(sel, 1.0, 0.0)
    ohb = onehot.astype(BF16)
    before = _dot(ohb, tri_ref[...])
    rank_o[...] = jnp.concatenate(
        [jnp.sum(jnp.where(sel, before, 0.0), axis=0, keepdims=True) for sel in sels], 0).astype(jnp.int32)
    cnt_o[...] = _dot(ohb, one_ref[...])


def _outproj(x, o0, o1, fno, ln0_g, ln0_b, gt1, sc2, sh2, w_top, w_bot, b_fno, ln1_g, ln1_b, w_rt, b_r, tri, ones_col):
    b, seq, d = x.shape
    c = o0.shape[-1]
    nt = seq // TB
    ntile = b * nt
    ne = w_rt.shape[0]
    const2 = lambda i, t: (0, 0)
    vec = pl.BlockSpec((1, d), const2)
    mod = pl.BlockSpec((None, 1, d), lambda i, t: (i, 0, 0))
    tok = lambda i, t: (0, i * nt + t)
    return pl.pallas_call(
        _out_kernel,
        grid=(b, nt),
        in_specs=[pl.BlockSpec((None, TB, d), lambda i, t: (i, t, 0)),
                  pl.BlockSpec((None, TB, c), lambda i, t: (i, t + 1, 0)),
                  pl.BlockSpec((None, TB, c), lambda i, t: (i, t + 1, 0)),
                  pl.BlockSpec((TB, c), lambda i, t: (t, i)),
                  vec, vec, mod, mod, mod,
                  pl.BlockSpec((c, d), const2),
                  pl.BlockSpec((c, d), const2),
                  pl.BlockSpec((1, c), const2),
                  vec, vec,
                  pl.BlockSpec((ne, d), const2),
                  pl.BlockSpec((ne, 1), const2),
                  pl.BlockSpec((TB, TB), const2),
                  pl.BlockSpec((TB, 128), const2)],
        out_specs=[pl.BlockSpec((None, TB, d), lambda i, t: (i, t, 0)),
                   pl.BlockSpec((TB * (d // LANES), LANES), lambda i, t: (i * nt + t, 0)),
                   pl.BlockSpec((TOP_K, TB), tok),
                   pl.BlockSpec((TOP_K, TB), tok),
                   pl.BlockSpec((TOP_K, TB), tok),
                   pl.BlockSpec((None, ne, 128), lambda i, t: (i * nt + t, 0, 0))],
        out_shape=[jax.ShapeDtypeStruct((b, seq, d), F32),
                   jax.ShapeDtypeStruct((b * seq * (d // LANES), LANES), F32),
                   jax.ShapeDtypeStruct((TOP_K, b * seq), jnp.int32),
                   jax.ShapeDtypeStruct((TOP_K, b * seq), F32),
                   jax.ShapeDtypeStruct((TOP_K, b * seq), jnp.int32),
                   jax.ShapeDtypeStruct((ntile, ne, 128), F32)],
        compiler_params=_cp(("parallel", "arbitrary")),
        name="outproj",
    )(x, o0, o1, fno, ln0_g, ln0_b, gt1, sc2, sh2, w_top, w_bot, b_fno, ln1_g, ln1_b, w_rt, b_r, tri, ones_col)


def _slot_kernel(idx_ref, rank_ref, base_ref, o_ref):
    idx = idx_ref[...]
    base = base_ref[...]
    rowid = lax.broadcasted_iota(jnp.int32, (base.shape[0], idx.shape[1]), 0)
    rows = [jnp.sum(jnp.where(rowid == idx[k:k + 1], base, 0), axis=0, keepdims=True) for k in range(TOP_K)]
    o_ref[...] = rank_ref[...] + jnp.concatenate(rows, 0)


def _slots(idx, rank, base):
    ntile, ne, _ = base.shape
    return pl.pallas_call(
        _slot_kernel,
        grid=(ntile,),
        in_specs=[pl.BlockSpec((TOP_K, TB), lambda i: (0, i)),
                  pl.BlockSpec((TOP_K, TB), lambda i: (0, i)),
                  pl.BlockSpec((None, ne, 1), lambda i: (i, 0, 0))],
        out_specs=pl.BlockSpec((None, TOP_K, TB), lambda i: (i, 0, 0)),
        out_shape=jax.ShapeDtypeStruct((ntile, TOP_K, TB), jnp.int32),
        compiler_params=_cp(("parallel",)),
        name="slot",
    )(idx, rank, base)


def _row_copy(src, src_row, dst, dst_row, sem, pieces):
    return pltpu.make_async_copy(src.at[pl.ds(pl.multiple_of(src_row * pieces, pieces), pieces)],
                                 dst.at[pl.ds(pl.multiple_of(dst_row * pieces, pieces), pieces)], sem)


def _dispatch_kernel(pad_ref, slot_hbm, h_ref, xs_ref, slot_smem, zrow_ref, sem, ssem):
    i = pl.program_id(0)
    pieces = zrow_ref.shape[0]
    cp = pltpu.make_async_copy(slot_hbm.at[i], slot_smem, ssem)
    cp.start()
    cp.wait()

    def issue(j, carry):
        for k in range(TOP_K):
            _row_copy(h_ref, j, xs_ref, slot_smem[k, j], sem, pieces).start(priority=k % 2)
        return carry

    lax.fori_loop(0, TB, issue, 0)

    def drain(j, carry):
        for k in range(TOP_K):
            _row_copy(h_ref, 0, xs_ref, 0, sem, pieces).wait()
        return carry

    lax.fori_loop(0, TB, drain, 0)

    @pl.when(i == pl.num_programs(0) - 1)
    def _():
        zrow_ref[...] = jnp.zeros_like(zrow_ref)

        def per_expert(e, carry):
            start = pad_ref[e]
            n = pad_ref[N_EXPERTS + e]

            def zi(q, c2):
                _row_copy(zrow_ref, 0, xs_ref, start + q, sem, pieces).start()
                return c2

            lax.fori_loop(0, n, zi, 0)

            def zw(q, c2):
                _row_copy(zrow_ref, 0, xs_ref, 0, sem, pieces).wait()
                return c2

            lax.fori_loop(0, n, zw, 0)
            return carry

        lax.fori_loop(0, N_EXPERTS, per_expert, 0)


def _dispatch(padinfo, slot, h2, n_slots, pieces):
    ntile = h2.shape[0] // (TB * pieces)
    return pl.pallas_call(
        _dispatch_kernel,
        grid_spec=pltpu.PrefetchScalarGridSpec(
            num_scalar_prefetch=1,
            grid=(ntile,),
            in_specs=[pl.BlockSpec(memory_space=pl.ANY),
                      pl.BlockSpec((TB * pieces, LANES), lambda i, pad: (i, 0))],
            out_specs=pl.BlockSpec(memory_space=pl.ANY),
            scratch_shapes=[pltpu.SMEM((TOP_K, TB), jnp.int32),
                            pltpu.VMEM((pieces, LANES), F32),
                            pltpu.SemaphoreType.DMA,
                            pltpu.SemaphoreType.DMA]),
        out_shape=jax.ShapeDtypeStruct((n_slots * pieces, LANES), F32),
        compiler_params=_cp(("arbitrary",)),
        name="dispatch",
    )(padinfo, slot, h2)


def _expert_kernel(be_ref, nu_ref, x_ref, w1_ref, b1_ref, w2_ref, b2_ref, perm_ref, o_ref, w1b_ref, w2b_ref):
    i = pl.program_id(0)
    prev = be_ref[jnp.maximum(i - 1, 0)]
    changed = jnp.logical_or(i == 0, be_ref[i] != prev)
    dff2 = w1_ref.shape[1]
    nblk = dff2 // 256

    @pl.when(changed)
    def _():
        perm = perm_ref[...]
        for j in range(nblk):
            cs = slice(j * 256, (j + 1) * 256)
            w1b_ref[:, cs] = _dot(w1_ref[:, cs].astype(BF16), perm).astype(BF16)
        w2b_ref[...] = w2_ref[...].astype(BF16)

    @pl.when(i < nu_ref[0])
    def _():
        x = _load_token_rows(x_ref, EBLK, w1_ref.shape[0] // LANES)
        u = _dot(x.astype(BF16), w1b_ref[...]) + b1_ref[...]
        acts = []
        for j in range(nblk):
            glu = jnp.minimum(u[:, j * 256:j * 256 + 128], SWIGLU_LIMIT)
            lin = jnp.clip(u[:, j * 256 + 128:(j + 1) * 256], -SWIGLU_LIMIT, SWIGLU_LIMIT)
            acts.append(glu * _sigmoid(SWIGLU_ALPHA * glu) * (lin + 1.0))
        act = jnp.concatenate(acts, 1).astype(BF16)
        _store_token_rows(o_ref, _dot(act, w2b_ref[...]) + b2_ref[...])

    @pl.when(i >= nu_ref[0])
    def _():
        o_ref[...] = jnp.zeros_like(o_ref)


def _experts(block_e, n_used, xs, w1, b1p, w2, b2, perm):
    ne, d, dff2 = w1.shape
    dff = w2.shape[1]
    pieces = d // LANES
    nblocks = xs.shape[0] // (EBLK * pieces)
    return pl.pallas_call(
        _expert_kernel,
        grid_spec=pltpu.PrefetchScalarGridSpec(
            num_scalar_prefetch=2,
            grid=(nblocks,),
            in_specs=[pl.BlockSpec((EBLK * pieces, LANES), lambda i, be, nu: (jnp.minimum(i, nu[0] - 1), 0)),
                      pl.BlockSpec((None, d, dff2), lambda i, be, nu: (be[i], 0, 0)),
                      pl.BlockSpec((None, 1, dff2), lambda i, be, nu: (be[i], 0, 0)),
                      pl.BlockSpec((None, dff, d), lambda i, be, nu: (be[i], 0, 0)),
                      pl.BlockSpec((None, 1, d), lambda i, be, nu: (be[i], 0, 0)),
                      pl.BlockSpec((256, 256), lambda i, be, nu: (0, 0))],
            out_specs=pl.BlockSpec((EBLK * pieces, LANES), lambda i, be, nu: (i, 0)),
            scratch_shapes=[pltpu.VMEM((d, dff2), BF16),
                            pltpu.VMEM((dff, d), BF16)]),
        out_shape=jax.ShapeDtypeStruct(xs.shape, F32),
        compiler_params=_cp(("arbitrary",)),
        name="experts",
    )(block_e, n_used, xs, w1, b1p, w2, b2, perm)


def _combine_kernel(slot_hbm, ys_ref, gate_ref, x1_ref, gt_ref, g_ref, b_ref, eye_ref, o_ref, slot_smem, buf_ref, sem, ssem):
    i = pl.program_id(0)
    pieces = o_ref.shape[1] // LANES
    cp = pltpu.make_async_copy(slot_hbm.at[i], slot_smem, ssem)
    cp.start()
    cp.wait()

    def issue(j, carry):
        for k in range(TOP_K):
            _row_copy(ys_ref, slot_smem[k, j], buf_ref.at[k], j, sem, pieces).start(priority=k % 2)
        return carry

    lax.fori_loop(0, TB, issue, 0)

    def drain(j, carry):
        for k in range(TOP_K):
            _row_copy(ys_ref, 0, buf_ref.at[k], 0, sem, pieces).wait()
        return carry

    lax.fori_loop(0, TB, drain, 0)

    gt = _dot_nt(eye_ref[...], gate_ref[...], precision=HIGHEST)
    f = gt[:, 0:1] * _load_token_rows(buf_ref.at[0], TB, pieces)
    for k in range(1, TOP_K):
        f = f + gt[:, k:k + 1] * _load_token_rows(buf_ref.at[k], TB, pieces)
    o_ref[...] = _ln(DEEPNORM_ALPHA * x1_ref[...] + gt_ref[...] * f, g_ref[...], b_ref[...])


def _combine(slot, ys, gates, x1, gt2, ln_g, ln_b, eye, tiles_per_batch):
    n, d = x1.shape
    ntile = n // TB
    return pl.pallas_call(
        _combine_kernel,
        grid=(ntile,),
        in_specs=[pl.BlockSpec(memory_space=pl.ANY),
                  pl.BlockSpec(memory_space=pl.ANY),
                  pl.BlockSpec((TOP_K, TB), lambda i: (0, i)),
                  pl.BlockSpec((TB, d), lambda i: (i, 0)),
                  pl.BlockSpec((None, 1, d), lambda i: (i // tiles_per_batch, 0, 0)),
                  pl.BlockSpec((1, d), lambda i: (0, 0)),
                  pl.BlockSpec((1, d), lambda i: (0, 0)),
                  pl.BlockSpec((TB, TB), lambda i: (0, 0))],
        out_specs=pl.BlockSpec((TB, d), lambda i: (i, 0)),
        out_shape=jax.ShapeDtypeStruct((n, d), F32),
        scratch_shapes=[pltpu.SMEM((TOP_K, TB), jnp.int32),
                        pltpu.VMEM((TOP_K, TB * (d // LANES), LANES), F32),
                        pltpu.SemaphoreType.DMA,
                        pltpu.SemaphoreType.DMA],
        compiler_params=_cp(("arbitrary",)),
        name="combine",
    )(slot, ys, gates, x1, gt2, ln_g, ln_b, eye)


def _scan_tables(reverse):
    t = np.arange(TB)
    same = (t[:, None] // CHUNK) == (t[None, :] // CHUNK)
    tri = (t[None, :] >= t[:, None]) if reverse else (t[None, :] <= t[:, None])
    return jnp.asarray(np.concatenate([same & tri, same], 0).astype(np.float32), BF16)


def _head_ones(c):
    h = np.arange(c) // HEAD
    return jnp.asarray((h[:, None] == h[None, :]).astype(np.float32), BF16)


def _deinterleave_perm():
    p = np.zeros((256, 256), np.float32)
    j = np.arange(128)
    p[2 * j, j] = 1.0
    p[2 * j + 1, 128 + j] = 1.0
    return jnp.asarray(p, BF16)


def kernel(x, c, ctx, c_ctx, ln0_g, ln0_b, w_ada, b_ada, w_in, mu_shift, w0, w2_decay, a0, a2_iclr, g2_gate, r_k, k_k, k_a, gn_g, gn_b, w_fno, b_fno, w_out, ln1_g, ln1_b, w_router, b_router, w1, b1, w2, b2, ln2_g, ln2_b):
    b, seq, d = x.shape
    ctx_len = ctx.shape[1]
    assert ctx_len == TB and seq % TB == 0 and w_ada.shape[0] == 1
    n_dir, cw = w0.shape[1], w0.shape[2]
    fgroups, gw = w_fno.shape[1], w_fno.shape[2]
    fw = fgroups * gw
    shift_w = mu_shift.shape[1]
    nd, na, ng = w2_decay.shape[2], a2_iclr.shape[2], g2_gate.shape[2]
    assert n_dir == 2 and shift_w == 3 * cw + 2 * (nd + na + ng)
    ne = w_router.shape[2]
    n_tok = b * seq
    row = lambda a: a.reshape(1, -1)

    rows = -(-(b + 1) // 8) * 8
    cc = jnp.zeros((rows, d), F32).at[:b].set(c).at[b].set(c_ctx)
    mod = _ada(cc, w_ada[0], b_ada[0])
    sh1, sc1, gt1, sh2, sc2, gt2 = [mod[:b, i * d:(i + 1) * d] for i in range(6)]
    sh1c, sc1c = mod[b, :d], mod[b, d:2 * d]
    scsel = jnp.stack([jnp.broadcast_to(sc1c, (b, d)), sc1], 1)[:, :, None, :]
    shsel = jnp.stack([jnp.broadcast_to(sh1c, (b, d)), sh1], 1)[:, :, None, :]
    mod3 = lambda a: a[:, None, :]

    cidx = np.arange(gw)
    ang = 2.0 * np.pi * ((cidx[:, None] * cidx[None, :]) % gw) / gw
    norm = 1.0 / math.sqrt(seq * gw)
    csc = jnp.asarray(np.stack([np.cos(ang) * norm, -np.sin(ang) * norm]), F32)
    win_f3 = w_in[0][:, shift_w:].reshape(d, fgroups, gw).transpose(1, 0, 2)
    wfc, wfs = _fold(csc, w_fno[0], win_f3)
    unf = lambda a: a.transpose(1, 0, 2).reshape(d, fw)
    w3 = jnp.concatenate([w_in[0][:, :shift_w], unf(wfc), unf(wfs)], 1).astype(BF16)

    s_all, fcat = _inproj(ctx, x, scsel, shsel, row(ln0_g), row(ln0_b), w3, shift_w, fw)

    def pad_dir(w, width):
        out = jnp.zeros((2, 2 * width, cw), F32)
        return out.at[0, :width].set(w[0]).at[1, width:].set(w[1]).astype(BF16)

    ones_bd = _head_ones(cw)
    r, v, kk, lw, kd, bb, g = _prep(
        s_all, row(mu_shift[0]), w0[0][:, None, :], pad_dir(w2_decay[0], nd), a0[0][:, None, :],
        pad_dir(a2_iclr[0], na), pad_dir(g2_gate[0], ng), row(k_k[0]), row(k_a[0]), ones_bd, cw)

    rk = r_k[0].reshape(2, 1, cw)
    outs = [_scan(r, v, kk, lw, kd, bb, g, rk, row(gn_g[0]), row(gn_b[0]), _scan_tables(dr == 1), ones_bd, dr)
            for dr in range(2)]

    fno = _matmul(_dft_matrix(seq), fcat.reshape(2 * seq, b * fw), 1024, 2048, 1024)

    ids = np.arange(TB)
    tri = jnp.asarray((ids[:, None] < ids[None, :]).astype(np.float32), BF16)
    ones_col = jnp.ones((TB, 128), BF16)
    wo = w_out[0].astype(BF16)
    x1, h2, idx, gates, rank, cnt = _outproj(
        x, outs[0], outs[1], fno, row(ln0_g), row(ln0_b), mod3(gt1), mod3(sc2), mod3(sh2), wo[:cw], wo[cw:],
        row(b_fno[0]), row(ln1_g[0]), row(ln1_b[0]), w_router[0].T, b_router[0].reshape(ne, 1), tri, ones_col)

    ntile = n_tok // TB
    cnt_t = cnt[:, :, 0].astype(jnp.int32)
    counts = jnp.sum(cnt_t, 0)
    padded = (counts + EBLK - 1) // EBLK * EBLK
    pends = jnp.cumsum(padded)
    pstarts = pends - padded
    base = (pstarts[None, :] + jnp.cumsum(cnt_t, 0) - cnt_t)[:, :, None]
    nblocks = (n_tok * TOP_K) // EBLK + ne
    n_slots = nblocks * EBLK
    block_start = jnp.arange(nblocks, dtype=jnp.int32) * EBLK
    block_e = jnp.minimum(jnp.sum((pends[None, :] <= block_start[:, None]).astype(jnp.int32), 1), ne - 1)
    n_used = (pends[-1:] // EBLK).astype(jnp.int32)
    padinfo = jnp.concatenate([pstarts + counts, padded - counts]).astype(jnp.int32)

    slot = _slots(idx, rank, base)
    xs = _dispatch(padinfo, slot, h2, n_slots, d // LANES)
    b1p = b1[0].reshape(ne, -1, 128, 2).transpose(0, 1, 3, 2).reshape(ne, 1, -1)
    ys = _experts(block_e, n_used, xs, w1[0], b1p, w2[0], b2[0][:, None, :], _deinterleave_perm())
    out = _combine(slot, ys, gates, x1.reshape(n_tok, d), mod3(gt2), row(ln2_g[0]), row(ln2_b[0]),
                   jnp.eye(TB, dtype=F32), seq // TB)
    return out.reshape(b, seq, d)
```

```python
import functools
import math

import numpy as np
import jax
import jax.numpy as jnp
from jax import lax
from jax.experimental import pallas as pl
from jax.experimental.pallas import tpu as pltpu

F32 = jnp.float32
BF16 = jnp.bfloat16
HIGHEST = lax.Precision.HIGHEST

GRID_W = 64
HEAD = 64
CHUNK = 64
TB = 256
N_EXPERTS = 32
TOP_K = 4
EBLK = 512
SWIGLU_LIMIT = 7.0
SWIGLU_ALPHA = 1.702
LN_EPS = 1e-5
GN_EPS = 64e-5
DEEPNORM_ALPHA = 2.0 ** 0.25
DECAY_SCALE = math.exp(-0.5)
LANES = 128
VMEM_LIMIT = 56 * 1024 * 1024


def _cp(sem, vmem=VMEM_LIMIT):
    return pltpu.CompilerParams(dimension_semantics=sem, vmem_limit_bytes=vmem)


def _sigmoid(x):
    return 1.0 / (1.0 + jnp.exp(-x))


def _ln(u, g, b):
    mean = jnp.mean(u, -1, keepdims=True)
    d = u - mean
    var = jnp.mean(d * d, -1, keepdims=True)
    return d * lax.rsqrt(var + LN_EPS) * g + b


def _dot(a, b):
    return jnp.dot(a, b, preferred_element_type=F32)


def _dot_nt(a, b, precision=None):
    return lax.dot_general(a, b, (((1,), (1,)), ((), ())), precision=precision, preferred_element_type=F32)


def _split_dot(x, w):
    hi = x.astype(BF16)
    lo = (x - hi.astype(F32)).astype(BF16)
    return _dot(hi, w) + _dot(lo, w)


def _load_token_rows(ref, n, pieces):
    return jnp.concatenate([ref[pl.ds(j, n, stride=pieces), :] for j in range(pieces)], 1)


def _store_token_rows(ref, val):
    n, width = val.shape
    pieces = width // LANES
    for j in range(pieces):
        ref[pl.ds(j, n, stride=pieces), :] = val[:, j * LANES:(j + 1) * LANES]


def _ada_kernel(c_ref, w_ref, b_ref, o_ref):
    c = c_ref[...]
    s = c * _sigmoid(c)
    o_ref[...] = jnp.dot(s, w_ref[...], precision=HIGHEST, preferred_element_type=F32) + b_ref[...]


def _ada(cc, w_ada, b_ada):
    rows, d = cc.shape
    n = w_ada.shape[1]
    tn = 1024
    return pl.pallas_call(
        _ada_kernel,
        grid=(n // tn,),
        in_specs=[pl.BlockSpec((rows, d), lambda j: (0, 0)),
                  pl.BlockSpec((d, tn), lambda j: (0, j)),
                  pl.BlockSpec((1, tn), lambda j: (0, j))],
        out_specs=pl.BlockSpec((rows, tn), lambda j: (0, j)),
        out_shape=jax.ShapeDtypeStruct((rows, n), F32),
        compiler_params=_cp(("parallel",)),
        name="ada",
    )(cc, w_ada, b_ada.reshape(1, n))


def _fold_kernel(cs_ref, wf_ref, win_ref, oc_ref, os_ref):
    wf = wf_ref[...]
    mc = jnp.dot(cs_ref[0], wf, precision=HIGHEST, preferred_element_type=F32)
    ms = jnp.dot(cs_ref[1], wf, precision=HIGHEST, preferred_element_type=F32)
    w = win_ref[...]
    oc_ref[...] = jnp.dot(w, mc, precision=HIGHEST, preferred_element_type=F32)
    os_ref[...] = jnp.dot(w, ms, precision=HIGHEST, preferred_element_type=F32)


def _fold(csc, w_fno, win_f3):
    g, d, gw = win_f3.shape
    return pl.pallas_call(
        _fold_kernel,
        grid=(g,),
        in_specs=[pl.BlockSpec((2, gw, gw), lambda i: (0, 0, 0)),
                  pl.BlockSpec((None, gw, gw), lambda i: (i, 0, 0)),
                  pl.BlockSpec((None, d, gw), lambda i: (i, 0, 0))],
        out_specs=[pl.BlockSpec((None, d, gw), lambda i: (i, 0, 0)),
                   pl.BlockSpec((None, d, gw), lambda i: (i, 0, 0))],
        out_shape=[jax.ShapeDtypeStruct((g, d, gw), F32)] * 2,
        compiler_params=_cp(("parallel",)),
        name="fold",
    )(csc, w_fno, win_f3)


def _in_kernel(ctx_ref, x_ref, sc_ref, sh_ref, g_ref, b_ref, w_ref, s_ref, f_ref, *, shift_w, fw):
    xin = jnp.where(pl.program_id(1) == 0, ctx_ref[...], x_ref[...])
    xn = _ln(xin, g_ref[...], b_ref[...])
    h = xn * (1.0 + sc_ref[...]) + sh_ref[...]
    p = _dot(h.astype(BF16), w_ref[...])
    s_ref[...] = p[:, :shift_w]
    f_ref[0] = p[:, shift_w:shift_w + fw].astype(BF16)
    f_ref[1] = p[:, shift_w + fw:].astype(BF16)


def _inproj(ctx, x, scsel, shsel, ln_g, ln_b, w3, shift_w, fw):
    b, seq, d = x.shape
    tall = ctx.shape[1] + seq
    nt = tall // TB
    wn = w3.shape[1]
    return pl.pallas_call(
        functools.partial(_in_kernel, shift_w=shift_w, fw=fw),
        grid=(b, nt),
        in_specs=[pl.BlockSpec((None, TB, d), lambda i, t: (i, 0, 0)),
                  pl.BlockSpec((None, TB, d), lambda i, t: (i, jnp.maximum(t - 1, 0), 0)),
                  pl.BlockSpec((None, None, 1, d), lambda i, t: (i, jnp.minimum(t, 1), 0, 0)),
                  pl.BlockSpec((None, None, 1, d), lambda i, t: (i, jnp.minimum(t, 1), 0, 0)),
                  pl.BlockSpec((1, d), lambda i, t: (0, 0)),
                  pl.BlockSpec((1, d), lambda i, t: (0, 0)),
                  pl.BlockSpec((d, wn), lambda i, t: (0, 0))],
        out_specs=[pl.BlockSpec((None, TB, shift_w), lambda i, t: (i, t, 0)),
                   pl.BlockSpec((2, TB, fw), lambda i, t: (0, jnp.maximum(t - 1, 0), i))],
        out_shape=[jax.ShapeDtypeStruct((b, tall, shift_w), F32),
                   jax.ShapeDtypeStruct((2, seq, b * fw), BF16)],
        compiler_params=_cp(("parallel", "arbitrary")),
        name="inproj",
    )(ctx, x, scsel, shsel, ln_g, ln_b, w3)


def _prep_kernel(s_ref, sp_ref, sn_ref, mu_ref, w0_ref, w2d_ref, a0_ref, a2_ref, g2_ref, kkw_ref, ka_ref, ones_ref,
                 r_o, v_o, kk_o, lw_o, kd_o, bb_o, g_o, *, c):
    t = pl.program_id(1)
    nt = pl.num_programs(1)
    s = s_ref[...]
    idx = lax.broadcasted_iota(jnp.int32, (TB, 1), 0)
    col = idx & (GRID_W - 1)
    is_ctx = t == 0
    lmask = jnp.where(is_ctx, idx, col) == 0
    rmask = jnp.where(is_ctx, idx - (TB - 1), col - (GRID_W - 1)) == 0
    left = jnp.where(lmask, 0.0, pltpu.roll(s, 1, 0))
    right = jnp.where(rmask, 0.0, pltpu.roll(s, TB - 1, 0))
    up = jnp.concatenate([jnp.where(t == 1, 0.0, sp_ref[...]), s[:TB - GRID_W]], 0)
    down = jnp.concatenate([s[GRID_W:], jnp.where(t == nt - 1, 0.0, sn_ref[...])], 0)
    ud = jnp.where(is_ctx, 0.0, up + down)
    sh = (ud + left + right) * jnp.where(is_ctx, 0.5, 0.25)
    m = s + mu_ref[...] * (sh - s)

    r = m[:, :c]
    k = m[:, c:2 * c]
    v = m[:, 2 * c:3 * c]
    o = 3 * c
    nd = w2d_ref.shape[1]
    na = a2_ref.shape[1]
    ng = g2_ref.shape[1]
    wd = jnp.tanh(m[:, o:o + nd]).astype(BF16)
    ad = m[:, o + nd:o + nd + na].astype(BF16)
    gd = _sigmoid(m[:, o + nd + na:o + nd + na + ng]).astype(BF16)
    kk = k * kkw_ref[...]
    ss = _split_dot(kk * kk, ones_ref[...])
    kk = kk / jnp.maximum(jnp.sqrt(ss), 1e-12)
    r_o[...] = r
    v_o[...] = v
    kk_o[...] = kk
    for d in range(2):
        wl = w0_ref[d] + _dot(wd, w2d_ref[d])
        lw_o[d] = -DECAY_SCALE * _sigmoid(wl)
        a = _sigmoid(a0_ref[d] + _dot(ad, a2_ref[d]))
        g_o[d] = _dot(gd, g2_ref[d])
        kd_o[d] = k * (1.0 + (a - 1.0) * ka_ref[...])
        bb_o[d] = kk * a


def _prep(s_all, mu, w0, w2d, a0, a2, g2, k_k, k_a, ones_bd, c):
    b, tall, sw = s_all.shape
    nt = tall // TB
    nhb = tall // GRID_W
    hb = TB // GRID_W
    const2 = lambda i, t: (0, 0)
    const3 = lambda i, t: (0, 0, 0)
    o1 = pl.BlockSpec((None, TB, c), lambda i, t: (i, t, 0))
    o2 = pl.BlockSpec((2, None, TB, c), lambda i, t: (0, i, t, 0))
    s1 = jax.ShapeDtypeStruct((b, tall, c), F32)
    s2 = jax.ShapeDtypeStruct((2, b, tall, c), F32)
    return pl.pallas_call(
        functools.partial(_prep_kernel, c=c),
        grid=(b, nt),
        in_specs=[pl.BlockSpec((None, TB, sw), lambda i, t: (i, t, 0)),
                  pl.BlockSpec((None, GRID_W, sw), lambda i, t: (i, jnp.maximum(t * hb - 1, 0), 0)),
                  pl.BlockSpec((None, GRID_W, sw), lambda i, t: (i, jnp.minimum(t * hb + hb, nhb - 1), 0)),
                  pl.BlockSpec((1, sw), const2),
                  pl.BlockSpec(w0.shape, const3),
                  pl.BlockSpec(w2d.shape, const3),
                  pl.BlockSpec(a0.shape, const3),
                  pl.BlockSpec(a2.shape, const3),
                  pl.BlockSpec(g2.shape, const3),
                  pl.BlockSpec((1, c), const2),
                  pl.BlockSpec((1, c), const2),
                  pl.BlockSpec((c, c), const2)],
        out_specs=[o1, o1, o1, o2, o2, o2, o2],
        out_shape=[s1, s1, s1, s2, s2, s2, s2],
        compiler_params=_cp(("parallel", "arbitrary")),
        name="prep",
    )(s_all, s_all, s_all, mu, w0, w2d, a0, a2, g2, k_k, k_a, ones_bd)


QUAD = 4 * HEAD


def _unit_triangular_inverses(ns):
    ti = lax.broadcasted_iota(jnp.int32, ns[0].shape, 0)
    tj = lax.broadcasted_iota(jnp.int32, ns[0].shape, 1)

    def same_block(shift):
        return (ti >> shift) == (tj >> shift)

    zero = jnp.zeros_like(ns[0])
    n8 = [jnp.where(same_block(3), n, zero) for n in ns]
    n8s = [_dot(a, a).astype(BF16) for a in n8]
    n8q = [_dot(a, a).astype(BF16) for a in n8s]
    xs = [jnp.where(ti == tj, jnp.ones_like(a), a) for a in n8]
    xs = [(x.astype(F32) + _dot(x, a)).astype(BF16) for x, a in zip(xs, n8s)]
    xs = [(x.astype(F32) + _dot(x, a)).astype(BF16) for x, a in zip(xs, n8q)]
    for shift in (3, 4, 5):
        offs = [jnp.where(same_block(shift + 1), jnp.where(same_block(shift), zero, n), zero) for n in ns]
        fs = [_dot(x, off).astype(BF16) for x, off in zip(xs, offs)]
        xs = [(x.astype(F32) + _dot(f, x)).astype(BF16) for x, f in zip(xs, fs)]
    return xs


def _scan_kernel(r_ref, v_ref, kk_ref, lw_ref, kd_ref, bb_ref, g_ref, rk_ref, gng_ref, gnb_ref, ts_ref, ones_ref,
                 o_ref, st_ref, y_ref, *, reverse):
    t = pl.program_id(1)
    width = r_ref.shape[-1]
    nquad = width // QUAD
    nchunk = TB // CHUNK
    nh = QUAD // HEAD

    @pl.when(t == 0)
    def _():
        st_ref[...] = jnp.zeros_like(st_ref)

    r = r_ref[...]
    v = v_ref[...]
    kk = kk_ref[...]
    lw = lw_ref[...]
    kd = kd_ref[...]
    bb = bb_ref[...]

    p1 = lw.astype(BF16)
    r1 = lw - p1.astype(F32)
    p2 = r1.astype(BF16)
    p3 = (r1 - p2.astype(F32)).astype(BF16)
    ts = ts_ref[...]
    acc = _dot(ts, p1) + _dot(ts, p2) + _dot(ts, p3)
    cl = acc[:TB]
    tot = acc[TB:]
    dec_in = jnp.exp(cl)
    dec_inv = jnp.exp(-cl)
    dec_ex = jnp.exp(cl - lw)
    dec_end = jnp.exp(tot - cl)
    dec_all = jnp.exp(tot)
    at = (-(kk * dec_ex)).astype(BF16)
    bt = (bb * dec_inv).astype(BF16)
    kt = (kd * dec_inv).astype(BF16)
    rt = (r * dec_in).astype(BF16)
    bh = (bb * dec_end).astype(BF16)
    kh = (kd * dec_end).astype(BF16)
    vb = v.astype(BF16)

    lane_head = lax.broadcasted_iota(jnp.int32, (CHUNK, QUAD), 1) // HEAD
    qi = lax.broadcasted_iota(jnp.int32, (QUAD, QUAD), 0)
    qj = lax.broadcasted_iota(jnp.int32, (QUAD, QUAD), 1)
    ti = qi & (CHUNK - 1)
    sj = qj & (CHUNK - 1)
    strict = (sj > ti) if reverse else (sj < ti)
    incl = (sj >= ti) if reverse else (sj <= ti)
    same_head = (qi // HEAD) == (qj // HEAD)

    def stack_heads(x):
        return jnp.concatenate([jnp.where(lane_head == h, x, jnp.zeros_like(x)) for h in range(nh)], 0)

    def fold_heads(x):
        out = x[:CHUNK]
        for h in range(1, nh):
            out = out + x[h * CHUNK:(h + 1) * CHUNK]
        return out

    units = [(c, q) for c in range(nchunk) for q in range(nquad)]
    pre = {}
    for c, q in units:
        rs = slice(c * CHUNK, (c + 1) * CHUNK)
        ls = slice(q * QUAD, (q + 1) * QUAD)
        a_st = stack_heads(at[rs, ls])
        r_st = stack_heads(rt[rs, ls])
        b_st = stack_heads(bt[rs, ls])
        k_st = stack_heads(kt[rs, ls])
        v_st = stack_heads(vb[rs, ls])
        qq = _dot_nt(jnp.concatenate([a_st, r_st], 0), jnp.concatenate([b_st, k_st], 0))
        n_ab = jnp.where(strict, qq[:QUAD, :QUAD], 0.0).astype(BF16)
        l_ak = jnp.where(strict, qq[:QUAD, QUAD:], 0.0).astype(BF16)
        m_r = jnp.concatenate([jnp.where(incl, qq[QUAD:, :QUAD], 0.0),
                               jnp.where(incl, qq[QUAD:, QUAD:], 0.0)], 1).astype(BF16)
        pre[(c, q)] = (a_st, v_st, n_ab, l_ak, m_r)
    inv = _unit_triangular_inverses([pre[u][2] for u in units])
    zq = jnp.zeros((CHUNK, QUAD), BF16)
    zf = jnp.zeros((CHUNK, QUAD), F32)
    rows = [slice(c * CHUNK, (c + 1) * CHUNK) for c, q in units]
    lanes = [slice(q * QUAD, (q + 1) * QUAD) for c, q in units]
    a_sts, v_sts, _, l_aks, m_rs = zip(*[pre[u] for u in units])
    lvs = [_dot(l_ak, v_st).astype(BF16) for l_ak, v_st in zip(l_aks, v_sts)]
    xws = [_dot(x, jnp.concatenate([a_st, lv], 1)) for x, a_st, lv in zip(inv, a_sts, lvs)]
    w_sts = [xw[:, :QUAD] for xw in xws]
    u0_sts = [xw[:, QUAD:] for xw in xws]
    rhats = [rt[rs, ls].astype(F32) + fold_heads(_dot(m_r[:, :QUAD], w_st.astype(BF16)))
             for rs, ls, m_r, w_st in zip(rows, lanes, m_rs, w_sts)]
    ycs = [fold_heads(_dot(m_r, jnp.concatenate([u0_st.astype(BF16), v_st], 0)))
           for m_r, u0_st, v_st in zip(m_rs, u0_sts, v_sts)]
    wuvs = [jnp.concatenate([fold_heads(w_st), fold_heads(u0_st), v[rs, ls], zf], 0).T.astype(BF16)
            for rs, ls, w_st, u0_st in zip(rows, lanes, w_sts, u0_sts)]
    ends = [jnp.concatenate([jnp.concatenate([bh[rs, ls], zq], 1), jnp.concatenate([zq, bh[rs, ls]], 1),
                             jnp.concatenate([zq, kh[rs, ls]], 1), jnp.concatenate([zq, zq], 1)], 0)
            for rs, ls in zip(rows, lanes)]
    ghs = [_dot(wuv, end) for wuv, end in zip(wuvs, ends)]
    fin = {}
    for u, rhat, yc, gh in zip(units, rhats, ycs, ghs):
        g = jnp.where(same_head, gh[:, :QUAD], 0.0).astype(BF16)
        hc = jnp.where(same_head, gh[:, QUAD:], 0.0)
        fin[u] = (rhat.astype(BF16), yc, g, hc)

    order = range(nchunk - 1, -1, -1) if reverse else range(nchunk)
    states = [st_ref[q] for q in range(nquad)]
    for c in order:
        rs = slice(c * CHUNK, (c + 1) * CHUNK)
        for q in range(nquad):
            ls = slice(q * QUAD, (q + 1) * QUAD)
            rhat, yc, g, hc = fin[(c, q)]
            st = states[q]
            stb = st.astype(BF16)
            y_ref[rs, ls] = _dot_nt(rhat, stb) + yc
            states[q] = st * dec_all[c * CHUNK:c * CHUNK + 1, ls] + _dot(stb, g) + hc
    for q in range(nquad):
        st_ref[q] = states[q]

    ones = ones_ref[...]
    inv_n = 1.0 / HEAD
    y = y_ref[...]
    mean = _split_dot(y, ones) * inv_n
    d = y - mean
    var = _split_dot(d * d, ones) * inv_n
    yn = d * lax.rsqrt(var + GN_EPS) * gng_ref[...] + gnb_ref[...]
    bonus = _split_dot(r * kd * rk_ref[...], ones) * v
    o_ref[...] = (yn + bonus) * g_ref[...]


def _scan(r, v, kk, lw, kd, bb, g, rk, gn_g, gn_b, ts, ones_bd, direction):
    b, tall, c = r.shape
    nt = tall // TB
    reverse = direction == 1
    if reverse:
        tmap = lambda t: jnp.where(t == 0, 0, nt - t)
    else:
        tmap = lambda t: t
    shared = pl.BlockSpec((None, TB, c), lambda i, t: (i, tmap(t), 0))
    perdir = pl.BlockSpec((None, None, TB, c), lambda i, t: (direction, i, tmap(t), 0))
    const2 = lambda i, t: (0, 0)
    return pl.pallas_call(
        functools.partial(_scan_kernel, reverse=reverse),
        grid=(b, nt),
        in_specs=[shared, shared, shared, perdir, perdir, perdir, perdir,
                  pl.BlockSpec((None, 1, c), lambda i, t: (direction, 0, 0)),
                  pl.BlockSpec((1, c), const2),
                  pl.BlockSpec((1, c), const2),
                  pl.BlockSpec(ts.shape, const2),
                  pl.BlockSpec((c, c), const2)],
        out_specs=pl.BlockSpec((None, TB, c), lambda i, t: (i, tmap(t), 0)),
        out_shape=jax.ShapeDtypeStruct((b, tall, c), F32),
        scratch_shapes=[pltpu.VMEM((c // QUAD, QUAD, QUAD), F32),
                        pltpu.VMEM((TB, c), F32)],
        compiler_params=_cp(("parallel", "arbitrary")),
        name="scan_rev" if reverse else "scan_fwd",
    )(r, v, kk, lw, kd, bb, g, rk, gn_g, gn_b, ts, ones_bd)


def _mm_kernel(a_ref, b_ref, o_ref, acc_ref):
    k = pl.program_id(2)

    @pl.when(k == 0)
    def _():
        acc_ref[...] = jnp.zeros_like(acc_ref)

    acc_ref[...] += _dot(a_ref[...], b_ref[...])

    @pl.when(k == pl.num_programs(2) - 1)
    def _():
        o_ref[...] = acc_ref[...]


def _matmul(a, b, tm, tn, tk):
    m, kd = a.shape
    n = b.shape[1]
    tm, tn, tk = min(tm, m), min(tn, n), min(tk, kd)
    return pl.pallas_call(
        _mm_kernel,
        grid=(m // tm, n // tn, kd // tk),
        in_specs=[pl.BlockSpec((tm, tk), lambda i, j, k: (i, k)),
                  pl.BlockSpec((tk, tn), lambda i, j, k: (k, j))],
        out_specs=pl.BlockSpec((tm, tn), lambda i, j, k: (i, j)),
        out_shape=jax.ShapeDtypeStruct((m, n), F32),
        scratch_shapes=[pltpu.VMEM((tm, tn), F32)],
        compiler_params=_cp(("parallel", "parallel", "arbitrary")),
        name="dft",
    )(a, b)


def _dft_matrix(seq):
    hi = seq // 128
    k = np.arange(seq, dtype=np.int64)[:, None]
    a_ang = 2.0 * np.pi * ((k * np.arange(hi)[None, :] * 128) % seq) / seq
    b_ang = 2.0 * np.pi * ((k * np.arange(128)[None, :]) % seq) / seq
    ca = jnp.asarray(np.cos(a_ang), F32)[:, :, None]
    sa = jnp.asarray(np.sin(a_ang), F32)[:, :, None]
    cb = jnp.asarray(np.cos(b_ang), F32)[:, None, :]
    sb = jnp.asarray(np.sin(b_ang), F32)[:, None, :]
    cos = (ca * cb - sa * sb).reshape(seq, seq)
    sin = (sa * cb + ca * sb).reshape(seq, seq)
    return jnp.concatenate([cos, sin], 1).astype(BF16)


def _out_kernel(x_ref, o0_ref, o1_ref, fn_ref, g0_ref, b0_ref, gt_ref, sc_ref, sh_ref, wt_ref, wb_ref, bf_ref,
                g1_ref, b1_ref, wr_ref, br_ref, tri_ref, one_ref,
                x1_o, h2_o, idx_o, gate_o, rank_o, cnt_o):
    xn = _ln(x_ref[...], g0_ref[...], b0_ref[...])
    rw = (o0_ref[...] + o1_ref[...]).astype(BF16)
    fn = (fn_ref[...] + bf_ref[...]).astype(BF16)
    mo = _dot(rw, wt_ref[...]) + _dot(fn, wb_ref[...])
    x1 = _ln(DEEPNORM_ALPHA * xn + gt_ref[...] * mo, g1_ref[...], b1_ref[...])
    h2 = x1 * (1.0 + sc_ref[...]) + sh_ref[...]
    x1_o[...] = x1
    _store_token_rows(h2_o, h2)

    vals = _dot_nt(wr_ref[...], h2, precision=HIGHEST) + br_ref[...]
    ne = vals.shape[0]
    rowid = lax.broadcasted_iota(jnp.int32, vals.shape, 0)
    sels, tops, idxs = [], [], []
    for _ in range(TOP_K):
        mx = jnp.max(vals, axis=0, keepdims=True)
        ix = jnp.min(jnp.where(vals == mx, rowid, ne), axis=0, keepdims=True)
        sel = rowid == ix
        vals = jnp.where(sel, -jnp.inf, vals)
        sels.append(sel)
        tops.append(mx)
        idxs.append(ix)
    ex = [jnp.exp(m - tops[0]) for m in tops]
    den = ex[0] + ex[1] + ex[2] + ex[3]
    gate_o[...] = jnp.concatenate([e / den for e in ex], 0)
    idx_o[...] = jnp.concatenate(idxs, 0)
    onehot = jnp.zeros(vals.shape, F32)
    for sel in sels:
        onehot = onehot + jnp.where(sel, 1.0, 0.0)
    ohb = onehot.astype(BF16)
    before = _dot(ohb, tri_ref[...])
    rank_o[...] = jnp.concatenate(
        [jnp.sum(jnp.where(sel, before, 0.0), axis=0, keepdims=True) for sel in sels], 0).astype(jnp.int32)
    cnt_o[...] = _dot(ohb, one_ref[...])


def _outproj(x, o0, o1, fno, ln0_g, ln0_b, gt1, sc2, sh2, w_top, w_bot, b_fno, ln1_g, ln1_b, w_rt, b_r, tri, ones_col):
    b, seq, d = x.shape
    c = o0.shape[-1]
    nt = seq // TB
    ntile = b * nt
    ne = w_rt.shape[0]
    const2 = lambda i, t: (0, 0)
    vec = pl.BlockSpec((1, d), const2)
    mod = pl.BlockSpec((None, 1, d), lambda i, t: (i, 0, 0))
    tok = lambda i, t: (0, i * nt + t)
    return pl.pallas_call(
        _out_kernel,
        grid=(b, nt),
        in_specs=[pl.BlockSpec((None, TB, d), lambda i, t: (i, t, 0)),
                  pl.BlockSpec((None, TB, c), lambda i, t: (i, t + 1, 0)),
                  pl.BlockSpec((None, TB, c), lambda i, t: (i, t + 1, 0)),
                  pl.BlockSpec((TB, c), lambda i, t: (t, i)),
                  vec, vec, mod, mod, mod,
                  pl.BlockSpec((c, d), const2),
                  pl.BlockSpec((c, d), const2),
                  pl.BlockSpec((1, c), const2),
                  vec, vec,
                  pl.BlockSpec((ne, d), const2),
                  pl.BlockSpec((ne, 1), const2),
                  pl.BlockSpec((TB, TB), const2),
                  pl.BlockSpec((TB, 128), const2)],
        out_specs=[pl.BlockSpec((None, TB, d), lambda i, t: (i, t, 0)),
                   pl.BlockSpec((TB * (d // LANES), LANES), lambda i, t: (i * nt + t, 0)),
                   pl.BlockSpec((TOP_K, TB), tok),
                   pl.BlockSpec((TOP_K, TB), tok),
                   pl.BlockSpec((TOP_K, TB), tok),
                   pl.BlockSpec((None, ne, 128), lambda i, t: (i * nt + t, 0, 0))],
        out_shape=[jax.ShapeDtypeStruct((b, seq, d), F32),
                   jax.ShapeDtypeStruct((b * seq * (d // LANES), LANES), F32),
                   jax.ShapeDtypeStruct((TOP_K, b * seq), jnp.int32),
                   jax.ShapeDtypeStruct((TOP_K, b * seq), F32),
                   jax.ShapeDtypeStruct((TOP_K, b * seq), jnp.int32),
                   jax.ShapeDtypeStruct((ntile, ne, 128), F32)],
        compiler_params=_cp(("parallel", "arbitrary")),
        name="outproj",
    )(x, o0, o1, fno, ln0_g, ln0_b, gt1, sc2, sh2, w_top, w_bot, b_fno, ln1_g, ln1_b, w_rt, b_r, tri, ones_col)


def _slot_kernel(idx_ref, rank_ref, base_ref, o_ref):
    idx = idx_ref[...]
    base = base_ref[...]
    rowid = lax.broadcasted_iota(jnp.int32, (base.shape[0], idx.shape[1]), 0)
    rows = [jnp.sum(jnp.where(rowid == idx[k:k + 1], base, 0), axis=0, keepdims=True) for k in range(TOP_K)]
    o_ref[...] = rank_ref[...] + jnp.concatenate(rows, 0)


def _slots(idx, rank, base):
    ntile, ne, _ = base.shape
    return pl.pallas_call(
        _slot_kernel,
        grid=(ntile,),
        in_specs=[pl.BlockSpec((TOP_K, TB), lambda i: (0, i)),
                  pl.BlockSpec((TOP_K, TB), lambda i: (0, i)),
                  pl.BlockSpec((None, ne, 1), lambda i: (i, 0, 0))],
        out_specs=pl.BlockSpec((None, TOP_K, TB), lambda i: (i, 0, 0)),
        out_shape=jax.ShapeDtypeStruct((ntile, TOP_K, TB), jnp.int32),
        compiler_params=_cp(("parallel",)),
        name="slot",
    )(idx, rank, base)


def _row_copy(src, src_row, dst, dst_row, sem, pieces):
    return pltpu.make_async_copy(src.at[pl.ds(pl.multiple_of(src_row * pieces, pieces), pieces)],
                                 dst.at[pl.ds(pl.multiple_of(dst_row * pieces, pieces), pieces)], sem)


def _slot_table(slot_hbm, slot_smem, ssem):
    i = pl.program_id(0)
    cur = i % 2

    def copy(tile, buf):
        return pltpu.make_async_copy(slot_hbm.at[tile], slot_smem.at[buf], ssem.at[buf])

    @pl.when(i == 0)
    def _():
        copy(0, 0).start()

    @pl.when(i + 1 < pl.num_programs(0))
    def _():
        copy(i + 1, 1 - cur).start()

    copy(i, cur).wait()
    return cur


def _dispatch_kernel(pad_ref, slot_hbm, h_ref, xs_ref, slot_smem, zrow_ref, sem, ssem):
    i = pl.program_id(0)
    pieces = zrow_ref.shape[0]
    cur = _slot_table(slot_hbm, slot_smem, ssem)

    def issue(j, carry):
        for k in range(TOP_K):
            _row_copy(h_ref, j, xs_ref, slot_smem[cur, k, j], sem, pieces).start(priority=k % 2)
        return carry

    lax.fori_loop(0, TB, issue, 0)

    def drain(j, carry):
        for k in range(TOP_K):
            _row_copy(h_ref, 0, xs_ref, 0, sem, pieces).wait()
        return carry

    lax.fori_loop(0, TB, drain, 0)

    @pl.when(i == pl.num_programs(0) - 1)
    def _():
        zrow_ref[...] = jnp.zeros_like(zrow_ref)

        def per_expert(e, carry):
            start = pad_ref[e]
            n = pad_ref[N_EXPERTS + e]

            def zi(q, c2):
                _row_copy(zrow_ref, 0, xs_ref, start + q, sem, pieces).start()
                return c2

            lax.fori_loop(0, n, zi, 0)

            def zw(q, c2):
                _row_copy(zrow_ref, 0, xs_ref, 0, sem, pieces).wait()
                return c2

            lax.fori_loop(0, n, zw, 0)
            return carry

        lax.fori_loop(0, N_EXPERTS, per_expert, 0)


def _dispatch(padinfo, slot, h2, n_slots, pieces):
    ntile = h2.shape[0] // (TB * pieces)
    return pl.pallas_call(
        _dispatch_kernel,
        grid_spec=pltpu.PrefetchScalarGridSpec(
            num_scalar_prefetch=1,
            grid=(ntile,),
            in_specs=[pl.BlockSpec(memory_space=pl.ANY),
                      pl.BlockSpec((TB * pieces, LANES), lambda i, pad: (i, 0))],
            out_specs=pl.BlockSpec(memory_space=pl.ANY),
            scratch_shapes=[pltpu.SMEM((2, TOP_K, TB), jnp.int32),
                            pltpu.VMEM((pieces, LANES), F32),
                            pltpu.SemaphoreType.DMA,
                            pltpu.SemaphoreType.DMA((2,))]),
        out_shape=jax.ShapeDtypeStruct((n_slots * pieces, LANES), F32),
        compiler_params=_cp(("arbitrary",)),
        name="dispatch",
    )(padinfo, slot, h2)


def _expert_kernel(be_ref, nu_ref, x_ref, w1_ref, b1_ref, w2_ref, b2_ref, perm_ref, o_ref, w1b_ref, w2b_ref):
    i = pl.program_id(0)
    prev = be_ref[jnp.maximum(i - 1, 0)]
    changed = jnp.logical_or(i == 0, be_ref[i] != prev)
    dff2 = w1_ref.shape[1]
    nblk = dff2 // 256

    @pl.when(changed)
    def _():
        perm = perm_ref[...]
        for j in range(nblk):
            cs = slice(j * 256, (j + 1) * 256)
            w1b_ref[:, cs] = _dot(w1_ref[:, cs].astype(BF16), perm).astype(BF16)
        w2b_ref[...] = w2_ref[...].astype(BF16)

    @pl.when(i < nu_ref[0])
    def _():
        x = _load_token_rows(x_ref, EBLK, w1_ref.shape[0] // LANES)
        u = _dot(x.astype(BF16), w1b_ref[...]) + b1_ref[...]
        acts = []
        for j in range(nblk):
            glu = jnp.minimum(u[:, j * 256:j * 256 + 128], SWIGLU_LIMIT)
            lin = jnp.clip(u[:, j * 256 + 128:(j + 1) * 256], -SWIGLU_LIMIT, SWIGLU_LIMIT)
            acts.append(glu * _sigmoid(SWIGLU_ALPHA * glu) * (lin + 1.0))
        act = jnp.concatenate(acts, 1).astype(BF16)
        _store_token_rows(o_ref, _dot(act, w2b_ref[...]) + b2_ref[...])

    @pl.when(i >= nu_ref[0])
    def _():
        o_ref[...] = jnp.zeros_like(o_ref)


def _experts(block_e, n_used, xs, w1, b1p, w2, b2, perm):
    ne, d, dff2 = w1.shape
    dff = w2.shape[1]
    pieces = d // LANES
    nblocks = xs.shape[0] // (EBLK * pieces)
    return pl.pallas_call(
        _expert_kernel,
        grid_spec=pltpu.PrefetchScalarGridSpec(
            num_scalar_prefetch=2,
            grid=(nblocks,),
            in_specs=[pl.BlockSpec((EBLK * pieces, LANES), lambda i, be, nu: (jnp.minimum(i, nu[0] - 1), 0)),
                      pl.BlockSpec((None, d, dff2), lambda i, be, nu: (be[i], 0, 0)),
                      pl.BlockSpec((None, 1, dff2), lambda i, be, nu: (be[i], 0, 0)),
                      pl.BlockSpec((None, dff, d), lambda i, be, nu: (be[i], 0, 0)),
                      pl.BlockSpec((None, 1, d), lambda i, be, nu: (be[i], 0, 0)),
                      pl.BlockSpec((256, 256), lambda i, be, nu: (0, 0))],
            out_specs=pl.BlockSpec((EBLK * pieces, LANES), lambda i, be, nu: (i, 0)),
            scratch_shapes=[pltpu.VMEM((d, dff2), BF16),
                            pltpu.VMEM((dff, d), BF16)]),
        out_shape=jax.ShapeDtypeStruct(xs.shape, F32),
        compiler_params=_cp(("arbitrary",)),
        name="experts",
    )(block_e, n_used, xs, w1, b1p, w2, b2, perm)


def _combine_kernel(slot_hbm, ys_ref, gate_ref, x1_ref, gt_ref, g_ref, b_ref, eye_ref, o_ref, slot_smem, buf_ref, sem, ssem):
    i = pl.program_id(0)
    pieces = o_ref.shape[1] // LANES
    cur = _slot_table(slot_hbm, slot_smem, ssem)

    def issue(j, carry):
        for k in range(TOP_K):
            _row_copy(ys_ref, slot_smem[cur, k, j], buf_ref.at[k], j, sem, pieces).start(priority=k % 2)
        return carry

    lax.fori_loop(0, TB, issue, 0)

    def drain(j, carry):
        for k in range(TOP_K):
            _row_copy(ys_ref, 0, buf_ref.at[k], 0, sem, pieces).wait()
        return carry

    lax.fori_loop(0, TB, drain, 0)

    gt = _dot_nt(eye_ref[...], gate_ref[...], precision=HIGHEST)
    f = gt[:, 0:1] * _load_token_rows(buf_ref.at[0], TB, pieces)
    for k in range(1, TOP_K):
        f = f + gt[:, k:k + 1] * _load_token_rows(buf_ref.at[k], TB, pieces)
    o_ref[...] = _ln(DEEPNORM_ALPHA * x1_ref[...] + gt_ref[...] * f, g_ref[...], b_ref[...])


def _combine(slot, ys, gates, x1, gt2, ln_g, ln_b, eye, tiles_per_batch):
    n, d = x1.shape
    ntile = n // TB
    return pl.pallas_call(
        _combine_kernel,
        grid=(ntile,),
        in_specs=[pl.BlockSpec(memory_space=pl.ANY),
                  pl.BlockSpec(memory_space=pl.ANY),
                  pl.BlockSpec((TOP_K, TB), lambda i: (0, i)),
                  pl.BlockSpec((TB, d), lambda i: (i, 0)),
                  pl.BlockSpec((None, 1, d), lambda i: (i // tiles_per_batch, 0, 0)),
                  pl.BlockSpec((1, d), lambda i: (0, 0)),
                  pl.BlockSpec((1, d), lambda i: (0, 0)),
                  pl.BlockSpec((TB, TB), lambda i: (0, 0))],
        out_specs=pl.BlockSpec((TB, d), lambda i: (i, 0)),
        out_shape=jax.ShapeDtypeStruct((n, d), F32),
        scratch_shapes=[pltpu.SMEM((2, TOP_K, TB), jnp.int32),
                        pltpu.VMEM((TOP_K, TB * (d // LANES), LANES), F32),
                        pltpu.SemaphoreType.DMA,
                        pltpu.SemaphoreType.DMA((2,))],
        compiler_params=_cp(("arbitrary",)),
        name="combine",
    )(slot, ys, gates, x1, gt2, ln_g, ln_b, eye)


def _scan_tables(reverse):
    t = np.arange(TB)
    same = (t[:, None] // CHUNK) == (t[None, :] // CHUNK)
    tri = (t[None, :] >= t[:, None]) if reverse else (t[None, :] <= t[:, None])
    return jnp.asarray(np.concatenate([same & tri, same], 0).astype(np.float32), BF16)


def _head_ones(c):
    h = np.arange(c) // HEAD
    return jnp.asarray((h[:, None] == h[None, :]).astype(np.float32), BF16)


def _deinterleave_perm():
    p = np.zeros((256, 256), np.float32)
    j = np.arange(128)
    p[2 * j, j] = 1.0
    p[2 * j + 1, 128 + j] = 1.0
    return jnp.asarray(p, BF16)


def kernel(x, c, ctx, c_ctx, ln0_g, ln0_b, w_ada, b_ada, w_in, mu_shift, w0, w2_decay, a0, a2_iclr, g2_gate, r_k, k_k, k_a, gn_g, gn_b, w_fno, b_fno, w_out, ln1_g, ln1_b, w_router, b_router, w1, b1, w2, b2, ln2_g, ln2_b):
    b, seq, d = x.shape
    ctx_len = ctx.shape[1]
    assert ctx_len == TB and seq % TB == 0 and w_ada.shape[0] == 1
    n_dir, cw = w0.shape[1], w0.shape[2]
    fgroups, gw = w_fno.shape[1], w_fno.shape[2]
    fw = fgroups * gw
    shift_w = mu_shift.shape[1]
    nd, na, ng = w2_decay.shape[2], a2_iclr.shape[2], g2_gate.shape[2]
    assert n_dir == 2 and shift_w == 3 * cw + 2 * (nd + na + ng)
    ne = w_router.shape[2]
    n_tok = b * seq
    row = lambda a: a.reshape(1, -1)

    rows = -(-(b + 1) // 8) * 8
    cc = jnp.zeros((rows, d), F32).at[:b].set(c).at[b].set(c_ctx)
    mod = _ada(cc, w_ada[0], b_ada[0])
    sh1, sc1, gt1, sh2, sc2, gt2 = [mod[:b, i * d:(i + 1) * d] for i in range(6)]
    sh1c, sc1c = mod[b, :d], mod[b, d:2 * d]
    scsel = jnp.stack([jnp.broadcast_to(sc1c, (b, d)), sc1], 1)[:, :, None, :]
    shsel = jnp.stack([jnp.broadcast_to(sh1c, (b, d)), sh1], 1)[:, :, None, :]
    mod3 = lambda a: a[:, None, :]

    cidx = np.arange(gw)
    ang = 2.0 * np.pi * ((cidx[:, None] * cidx[None, :]) % gw) / gw
    norm = 1.0 / math.sqrt(seq * gw)
    csc = jnp.asarray(np.stack([np.cos(ang) * norm, -np.sin(ang) * norm]), F32)
    win_f3 = w_in[0][:, shift_w:].reshape(d, fgroups, gw).transpose(1, 0, 2)
    wfc, wfs = _fold(csc, w_fno[0], win_f3)
    unf = lambda a: a.transpose(1, 0, 2).reshape(d, fw)
    w3 = jnp.concatenate([w_in[0][:, :shift_w], unf(wfc), unf(wfs)], 1).astype(BF16)

    s_all, fcat = _inproj(ctx, x, scsel, shsel, row(ln0_g), row(ln0_b), w3, shift_w, fw)

    def pad_dir(w, width):
        out = jnp.zeros((2, 2 * width, cw), F32)
        return out.at[0, :width].set(w[0]).at[1, width:].set(w[1]).astype(BF16)

    ones_bd = _head_ones(cw)
    r, v, kk, lw, kd, bb, g = _prep(
        s_all, row(mu_shift[0]), w0[0][:, None, :], pad_dir(w2_decay[0], nd), a0[0][:, None, :],
        pad_dir(a2_iclr[0], na), pad_dir(g2_gate[0], ng), row(k_k[0]), row(k_a[0]), ones_bd, cw)

    rk = r_k[0].reshape(2, 1, cw)
    outs = [_scan(r, v, kk, lw, kd, bb, g, rk, row(gn_g[0]), row(gn_b[0]), _scan_tables(dr == 1), ones_bd, dr)
            for dr in range(2)]

    fno = _matmul(_dft_matrix(seq), fcat.reshape(2 * seq, b * fw), 1024, 2048, 1024)

    ids = np.arange(TB)
    tri = jnp.asarray((ids[:, None] < ids[None, :]).astype(np.float32), BF16)
    ones_col = jnp.ones((TB, 128), BF16)
    wo = w_out[0].astype(BF16)
    x1, h2, idx, gates, rank, cnt = _outproj(
        x, outs[0], outs[1], fno, row(ln0_g), row(ln0_b), mod3(gt1), mod3(sc2), mod3(sh2), wo[:cw], wo[cw:],
        row(b_fno[0]), row(ln1_g[0]), row(ln1_b[0]), w_router[0].T, b_router[0].reshape(ne, 1), tri, ones_col)

    ntile = n_tok // TB
    cnt_t = cnt[:, :, 0].astype(jnp.int32)
    counts = jnp.sum(cnt_t, 0)
    padded = (counts + EBLK - 1) // EBLK * EBLK
    pends = jnp.cumsum(padded)
    pstarts = pends - padded
    base = (pstarts[None, :] + jnp.cumsum(cnt_t, 0) - cnt_t)[:, :, None]
    nblocks = (n_tok * TOP_K) // EBLK + ne
    n_slots = nblocks * EBLK
    block_start = jnp.arange(nblocks, dtype=jnp.int32) * EBLK
    block_e = jnp.minimum(jnp.sum((pends[None, :] <= block_start[:, None]).astype(jnp.int32), 1), ne - 1)
    n_used = (pends[-1:] // EBLK).astype(jnp.int32)
    padinfo = jnp.concatenate([pstarts + counts, padded - counts]).astype(jnp.int32)

    slot = _slots(idx, rank, base)
    xs = _dispatch(padinfo, slot, h2, n_slots, d // LANES)
    b1p = b1[0].reshape(ne, -1, 128, 2).transpose(0, 1, 3, 2).reshape(ne, 1, -1)
    ys = _experts(block_e, n_used, xs, w1[0], b1p, w2[0], b2[0][:, None, :], _deinterleave_perm())
    out = _combine(slot, ys, gates, x1.reshape(n_tok, d), mod3(gt2), row(ln2_g[0]), row(ln2_b[0]),
                   jnp.eye(TB, dtype=F32), seq // TB)
    return out.reshape(b, seq, d)
```

```python
import functools
import math

import numpy as np
import jax
import jax.numpy as jnp
from jax import lax
from jax.experimental import pallas as pl
from jax.experimental.pallas import tpu as pltpu

F32 = jnp.float32
BF16 = jnp.bfloat16
HIGHEST = lax.Precision.HIGHEST

GRID_W = 64
HEAD = 64
CHUNK = 64
TB = 256
N_EXPERTS = 32
TOP_K = 4
EBLK = 512
SWIGLU_LIMIT = 7.0
SWIGLU_ALPHA = 1.702
LN_EPS = 1e-5
GN_EPS = 64e-5
DEEPNORM_ALPHA = 2.0 ** 0.25
DECAY_SCALE = math.exp(-0.5)
LANES = 128
VMEM_LIMIT = 56 * 1024 * 1024


def _cp(sem, vmem=VMEM_LIMIT):
    return pltpu.CompilerParams(dimension_semantics=sem, vmem_limit_bytes=vmem)


def _sigmoid(x):
    return 1.0 / (1.0 + jnp.exp(-x))


def _ln(u, g, b):
    mean = jnp.mean(u, -1, keepdims=True)
    d = u - mean
    var = jnp.mean(d * d, -1, keepdims=True)
    return d * lax.rsqrt(var + LN_EPS) * g + b


def _dot(a, b):
    return jnp.dot(a, b, preferred_element_type=F32)


def _dot_nt(a, b, precision=None):
    return lax.dot_general(a, b, (((1,), (1,)), ((), ())), precision=precision, preferred_element_type=F32)


def _split_dot(x, w):
    hi = x.astype(BF16)
    lo = (x - hi.astype(F32)).astype(BF16)
    return _dot(hi, w) + _dot(lo, w)


def _load_token_rows(ref, n, pieces):
    return jnp.concatenate([ref[pl.ds(j, n, stride=pieces), :] for j in range(pieces)], 1)


def _store_token_rows(ref, val):
    n, width = val.shape
    pieces = width // LANES
    for j in range(pieces):
        ref[pl.ds(j, n, stride=pieces), :] = val[:, j * LANES:(j + 1) * LANES]


def _ada_kernel(c_ref, w_ref, b_ref, o_ref):
    c = c_ref[...]
    s = c * _sigmoid(c)
    o_ref[...] = jnp.dot(s, w_ref[...], precision=HIGHEST, preferred_element_type=F32) + b_ref[...]


def _ada(cc, w_ada, b_ada):
    rows, d = cc.shape
    n = w_ada.shape[1]
    tn = 1024
    return pl.pallas_call(
        _ada_kernel,
        grid=(n // tn,),
        in_specs=[pl.BlockSpec((rows, d), lambda j: (0, 0)),
                  pl.BlockSpec((d, tn), lambda j: (0, j)),
                  pl.BlockSpec((1, tn), lambda j: (0, j))],
        out_specs=pl.BlockSpec((rows, tn), lambda j: (0, j)),
        out_shape=jax.ShapeDtypeStruct((rows, n), F32),
        compiler_params=_cp(("parallel",)),
        name="ada",
    )(cc, w_ada, b_ada.reshape(1, n))


def _fold_kernel(cs_ref, wf_ref, win_ref, oc_ref, os_ref):
    wf = wf_ref[...]
    mc = jnp.dot(cs_ref[0], wf, precision=HIGHEST, preferred_element_type=F32)
    ms = jnp.dot(cs_ref[1], wf, precision=HIGHEST, preferred_element_type=F32)
    w = win_ref[...]
    oc_ref[...] = jnp.dot(w, mc, precision=HIGHEST, preferred_element_type=F32)
    os_ref[...] = jnp.dot(w, ms, precision=HIGHEST, preferred_element_type=F32)


def _fold(csc, w_fno, win_f3):
    g, d, gw = win_f3.shape
    return pl.pallas_call(
        _fold_kernel,
        grid=(g,),
        in_specs=[pl.BlockSpec((2, gw, gw), lambda i: (0, 0, 0)),
                  pl.BlockSpec((None, gw, gw), lambda i: (i, 0, 0)),
                  pl.BlockSpec((None, d, gw), lambda i: (i, 0, 0))],
        out_specs=[pl.BlockSpec((None, d, gw), lambda i: (i, 0, 0)),
                   pl.BlockSpec((None, d, gw), lambda i: (i, 0, 0))],
        out_shape=[jax.ShapeDtypeStruct((g, d, gw), F32)] * 2,
        compiler_params=_cp(("parallel",)),
        name="fold",
    )(csc, w_fno, win_f3)


def _in_kernel(ctx_ref, x_ref, sc_ref, sh_ref, g_ref, b_ref, w_ref, s_ref, f_ref, *, shift_w, fw):
    xin = jnp.where(pl.program_id(1) == 0, ctx_ref[...], x_ref[...])
    xn = _ln(xin, g_ref[...], b_ref[...])
    h = xn * (1.0 + sc_ref[...]) + sh_ref[...]
    p = _dot(h.astype(BF16), w_ref[...])
    s_ref[...] = p[:, :shift_w]
    f_ref[0] = p[:, shift_w:shift_w + fw].astype(BF16)
    f_ref[1] = p[:, shift_w + fw:].astype(BF16)


def _inproj(ctx, x, scsel, shsel, ln_g, ln_b, w3, shift_w, fw):
    b, seq, d = x.shape
    tall = ctx.shape[1] + seq
    nt = tall // TB
    wn = w3.shape[1]
    return pl.pallas_call(
        functools.partial(_in_kernel, shift_w=shift_w, fw=fw),
        grid=(b, nt),
        in_specs=[pl.BlockSpec((None, TB, d), lambda i, t: (i, 0, 0)),
                  pl.BlockSpec((None, TB, d), lambda i, t: (i, jnp.maximum(t - 1, 0), 0)),
                  pl.BlockSpec((None, None, 1, d), lambda i, t: (i, jnp.minimum(t, 1), 0, 0)),
                  pl.BlockSpec((None, None, 1, d), lambda i, t: (i, jnp.minimum(t, 1), 0, 0)),
                  pl.BlockSpec((1, d), lambda i, t: (0, 0)),
                  pl.BlockSpec((1, d), lambda i, t: (0, 0)),
                  pl.BlockSpec((d, wn), lambda i, t: (0, 0))],
        out_specs=[pl.BlockSpec((None, TB, shift_w), lambda i, t: (i, t, 0)),
                   pl.BlockSpec((2, TB, fw), lambda i, t: (0, jnp.maximum(t - 1, 0), i))],
        out_shape=[jax.ShapeDtypeStruct((b, tall, shift_w), F32),
                   jax.ShapeDtypeStruct((2, seq, b * fw), BF16)],
        compiler_params=_cp(("parallel", "arbitrary")),
        name="inproj",
    )(ctx, x, scsel, shsel, ln_g, ln_b, w3)


def _prep_kernel(s_ref, sp_ref, sn_ref, mu_ref, w0_ref, w2d_ref, a0_ref, a2_ref, g2_ref, kkw_ref, ka_ref, ones_ref,
                 r_o, v_o, kk_o, lw_o, kd_o, bb_o, g_o, *, c):
    t = pl.program_id(1)
    nt = pl.num_programs(1)
    s = s_ref[...]
    idx = lax.broadcasted_iota(jnp.int32, (TB, 1), 0)
    col = idx & (GRID_W - 1)
    is_ctx = t == 0
    lmask = jnp.where(is_ctx, idx, col) == 0
    rmask = jnp.where(is_ctx, idx - (TB - 1), col - (GRID_W - 1)) == 0
    left = jnp.where(lmask, 0.0, pltpu.roll(s, 1, 0))
    right = jnp.where(rmask, 0.0, pltpu.roll(s, TB - 1, 0))
    up = jnp.concatenate([jnp.where(t == 1, 0.0, sp_ref[...]), s[:TB - GRID_W]], 0)
    down = jnp.concatenate([s[GRID_W:], jnp.where(t == nt - 1, 0.0, sn_ref[...])], 0)
    ud = jnp.where(is_ctx, 0.0, up + down)
    sh = (ud + left + right) * jnp.where(is_ctx, 0.5, 0.25)
    m = s + mu_ref[...] * (sh - s)

    r = m[:, :c]
    k = m[:, c:2 * c]
    v = m[:, 2 * c:3 * c]
    o = 3 * c
    nd = w2d_ref.shape[1]
    na = a2_ref.shape[1]
    ng = g2_ref.shape[1]
    wd = jnp.tanh(m[:, o:o + nd]).astype(BF16)
    ad = m[:, o + nd:o + nd + na].astype(BF16)
    gd = _sigmoid(m[:, o + nd + na:o + nd + na + ng]).astype(BF16)
    kk = k * kkw_ref[...]
    ss = _split_dot(kk * kk, ones_ref[...])
    kk = kk / jnp.maximum(jnp.sqrt(ss), 1e-12)
    r_o[...] = r
    v_o[...] = v
    kk_o[...] = kk
    for d in range(2):
        wl = w0_ref[d] + _dot(wd, w2d_ref[d])
        lw_o[d] = -DECAY_SCALE * _sigmoid(wl)
        a = _sigmoid(a0_ref[d] + _dot(ad, a2_ref[d]))
        g_o[d] = _dot(gd, g2_ref[d])
        kd_o[d] = k * (1.0 + (a - 1.0) * ka_ref[...])
        bb_o[d] = kk * a


def _prep(s_all, mu, w0, w2d, a0, a2, g2, k_k, k_a, ones_bd, c):
    b, tall, sw = s_all.shape
    nt = tall // TB
    nhb = tall // GRID_W
    hb = TB // GRID_W
    const2 = lambda i, t: (0, 0)
    const3 = lambda i, t: (0, 0, 0)
    o1 = pl.BlockSpec((None, TB, c), lambda i, t: (i, t, 0))
    o2 = pl.BlockSpec((2, None, TB, c), lambda i, t: (0, i, t, 0))
    s1 = jax.ShapeDtypeStruct((b, tall, c), F32)
    s2 = jax.ShapeDtypeStruct((2, b, tall, c), F32)
    return pl.pallas_call(
        functools.partial(_prep_kernel, c=c),
        grid=(b, nt),
        in_specs=[pl.BlockSpec((None, TB, sw), lambda i, t: (i, t, 0)),
                  pl.BlockSpec((None, GRID_W, sw), lambda i, t: (i, jnp.maximum(t * hb - 1, 0), 0)),
                  pl.BlockSpec((None, GRID_W, sw), lambda i, t: (i, jnp.minimum(t * hb + hb, nhb - 1), 0)),
                  pl.BlockSpec((1, sw), const2),
                  pl.BlockSpec(w0.shape, const3),
                  pl.BlockSpec(w2d.shape, const3),
                  pl.BlockSpec(a0.shape, const3),
                  pl.BlockSpec(a2.shape, const3),
                  pl.BlockSpec(g2.shape, const3),
                  pl.BlockSpec((1, c), const2),
                  pl.BlockSpec((1, c), const2),
                  pl.BlockSpec((c, c), const2)],
        out_specs=[o1, o1, o1, o2, o2, o2, o2],
        out_shape=[s1, s1, s1, s2, s2, s2, s2],
        compiler_params=_cp(("parallel", "arbitrary")),
        name="prep",
    )(s_all, s_all, s_all, mu, w0, w2d, a0, a2, g2, k_k, k_a, ones_bd)


QUAD = 4 * HEAD


def _unit_triangular_inverses(ns):
    ti = lax.broadcasted_iota(jnp.int32, ns[0].shape, 0)
    tj = lax.broadcasted_iota(jnp.int32, ns[0].shape, 1)

    def same_block(shift):
        return (ti >> shift) == (tj >> shift)

    zero = jnp.zeros_like(ns[0])
    n8 = [jnp.where(same_block(3), n, zero) for n in ns]
    n8s = [_dot(a, a).astype(BF16) for a in n8]
    n8q = [_dot(a, a).astype(BF16) for a in n8s]
    xs = [jnp.where(ti == tj, jnp.ones_like(a), a) for a in n8]
    xs = [(x.astype(F32) + _dot(x, a)).astype(BF16) for x, a in zip(xs, n8s)]
    xs = [(x.astype(F32) + _dot(x, a)).astype(BF16) for x, a in zip(xs, n8q)]
    for shift in (3, 4, 5):
        offs = [jnp.where(same_block(shift + 1), jnp.where(same_block(shift), zero, n), zero) for n in ns]
        fs = [_dot(x, off).astype(BF16) for x, off in zip(xs, offs)]
        xs = [(x.astype(F32) + _dot(f, x)).astype(BF16) for x, f in zip(xs, fs)]
    return xs


def _scan_kernel(r_ref, v_ref, kk_ref, lw_ref, kd_ref, bb_ref, g_ref, rk_ref, gng_ref, gnb_ref, ts_ref, ones_ref,
                 o_ref, st_ref, y_ref, *, reverse):
    t = pl.program_id(1)
    width = r_ref.shape[-1]
    nquad = width // QUAD
    nchunk = TB // CHUNK
    nh = QUAD // HEAD

    @pl.when(t == 0)
    def _():
        st_ref[...] = jnp.zeros_like(st_ref)

    r = r_ref[...]
    v = v_ref[...]
    kk = kk_ref[...]
    lw = lw_ref[...]
    kd = kd_ref[...]
    bb = bb_ref[...]

    p1 = lw.astype(BF16)
    r1 = lw - p1.astype(F32)
    p2 = r1.astype(BF16)
    p3 = (r1 - p2.astype(F32)).astype(BF16)
    ts = ts_ref[...]
    acc = _dot(ts, p1) + _dot(ts, p2) + _dot(ts, p3)
    cl = acc[:TB]
    tot = acc[TB:]
    dec_in = jnp.exp(cl)
    dec_inv = jnp.exp(-cl)
    dec_ex = jnp.exp(cl - lw)
    dec_end = jnp.exp(tot - cl)
    dec_all = jnp.exp(tot)
    at = (-(kk * dec_ex)).astype(BF16)
    bt = (bb * dec_inv).astype(BF16)
    kt = (kd * dec_inv).astype(BF16)
    rt = (r * dec_in).astype(BF16)
    bh = (bb * dec_end).astype(BF16)
    kh = (kd * dec_end).astype(BF16)
    vb = v.astype(BF16)

    lane_head = lax.broadcasted_iota(jnp.int32, (CHUNK, QUAD), 1) // HEAD
    qi = lax.broadcasted_iota(jnp.int32, (QUAD, QUAD), 0)
    qj = lax.broadcasted_iota(jnp.int32, (QUAD, QUAD), 1)
    ti = qi & (CHUNK - 1)
    sj = qj & (CHUNK - 1)
    strict = (sj > ti) if reverse else (sj < ti)
    incl = (sj >= ti) if reverse else (sj <= ti)
    same_head = (qi // HEAD) == (qj // HEAD)

    def stack_heads(x):
        return jnp.concatenate([jnp.where(lane_head == h, x, jnp.zeros_like(x)) for h in range(nh)], 0)

    def fold_heads(x):
        out = x[:CHUNK]
        for h in range(1, nh):
            out = out + x[h * CHUNK:(h + 1) * CHUNK]
        return out

    units = [(c, q) for c in range(nchunk) for q in range(nquad)]
    pre = {}
    for c, q in units:
        rs = slice(c * CHUNK, (c + 1) * CHUNK)
        ls = slice(q * QUAD, (q + 1) * QUAD)
        a_st = stack_heads(at[rs, ls])
        r_st = stack_heads(rt[rs, ls])
        b_st = stack_heads(bt[rs, ls])
        k_st = stack_heads(kt[rs, ls])
        v_st = stack_heads(vb[rs, ls])
        qq = _dot_nt(jnp.concatenate([a_st, r_st], 0), jnp.concatenate([b_st, k_st], 0))
        n_ab = jnp.where(strict, qq[:QUAD, :QUAD], 0.0).astype(BF16)
        l_ak = jnp.where(strict, qq[:QUAD, QUAD:], 0.0).astype(BF16)
        m_r = jnp.concatenate([jnp.where(incl, qq[QUAD:, :QUAD], 0.0),
                               jnp.where(incl, qq[QUAD:, QUAD:], 0.0)], 1).astype(BF16)
        pre[(c, q)] = (a_st, v_st, n_ab, l_ak, m_r)
    inv = _unit_triangular_inverses([pre[u][2] for u in units])
    zq = jnp.zeros((CHUNK, QUAD), BF16)
    zf = jnp.zeros((CHUNK, QUAD), F32)
    rows = [slice(c * CHUNK, (c + 1) * CHUNK) for c, q in units]
    lanes = [slice(q * QUAD, (q + 1) * QUAD) for c, q in units]
    a_sts, v_sts, _, l_aks, m_rs = zip(*[pre[u] for u in units])
    lvs = [_dot(l_ak, v_st).astype(BF16) for l_ak, v_st in zip(l_aks, v_sts)]
    xws = [_dot(x, jnp.concatenate([a_st, lv], 1)) for x, a_st, lv in zip(inv, a_sts, lvs)]
    w_sts = [xw[:, :QUAD] for xw in xws]
    u0_sts = [xw[:, QUAD:] for xw in xws]
    rhats = [rt[rs, ls].astype(F32) + fold_heads(_dot(m_r[:, :QUAD], w_st.astype(BF16)))
             for rs, ls, m_r, w_st in zip(rows, lanes, m_rs, w_sts)]
    ycs = [fold_heads(_dot(m_r, jnp.concatenate([u0_st.astype(BF16), v_st], 0)))
           for m_r, u0_st, v_st in zip(m_rs, u0_sts, v_sts)]
    wuvs = [jnp.concatenate([fold_heads(w_st), fold_heads(u0_st), v[rs, ls], zf], 0).T.astype(BF16)
            for rs, ls, w_st, u0_st in zip(rows, lanes, w_sts, u0_sts)]
    ends = [jnp.concatenate([jnp.concatenate([bh[rs, ls], zq], 1), jnp.concatenate([zq, bh[rs, ls]], 1),
                             jnp.concatenate([zq, kh[rs, ls]], 1), jnp.concatenate([zq, zq], 1)], 0)
            for rs, ls in zip(rows, lanes)]
    ghs = [_dot(wuv, end) for wuv, end in zip(wuvs, ends)]
    fin = {}
    for u, rhat, yc, gh in zip(units, rhats, ycs, ghs):
        g = jnp.where(same_head, gh[:, :QUAD], 0.0).astype(BF16)
        hc = jnp.where(same_head, gh[:, QUAD:], 0.0)
        fin[u] = (rhat.astype(BF16), yc, g, hc)

    order = range(nchunk - 1, -1, -1) if reverse else range(nchunk)
    states = [st_ref[q] for q in range(nquad)]
    for c in order:
        rs = slice(c * CHUNK, (c + 1) * CHUNK)
        for q in range(nquad):
            ls = slice(q * QUAD, (q + 1) * QUAD)
            rhat, yc, g, hc = fin[(c, q)]
            st = states[q]
            stb = st.astype(BF16)
            y_ref[rs, ls] = _dot_nt(rhat, stb) + yc
            states[q] = st * dec_all[c * CHUNK:c * CHUNK + 1, ls] + _dot(stb, g) + hc
    for q in range(nquad):
        st_ref[q] = states[q]

    ones = ones_ref[:QUAD, :QUAD]

    def head_sum(x):
        return jnp.concatenate([_split_dot(x[:, q * QUAD:(q + 1) * QUAD], ones) for q in range(nquad)], 1)

    inv_n = 1.0 / HEAD
    y = y_ref[...]
    mean = head_sum(y) * inv_n
    d = y - mean
    var = head_sum(d * d) * inv_n
    yn = d * lax.rsqrt(var + GN_EPS) * gng_ref[...] + gnb_ref[...]
    bonus = head_sum(r * kd * rk_ref[...]) * v
    o_ref[...] = (yn + bonus) * g_ref[...]


def _scan(r, v, kk, lw, kd, bb, g, rk, gn_g, gn_b, ts, ones_bd, direction):
    b, tall, c = r.shape
    nt = tall // TB
    reverse = direction == 1
    if reverse:
        tmap = lambda t: jnp.where(t == 0, 0, nt - t)
    else:
        tmap = lambda t: t
    shared = pl.BlockSpec((None, TB, c), lambda i, t: (i, tmap(t), 0))
    perdir = pl.BlockSpec((None, None, TB, c), lambda i, t: (direction, i, tmap(t), 0))
    const2 = lambda i, t: (0, 0)
    return pl.pallas_call(
        functools.partial(_scan_kernel, reverse=reverse),
        grid=(b, nt),
        in_specs=[shared, shared, shared, perdir, perdir, perdir, perdir,
                  pl.BlockSpec((None, 1, c), lambda i, t: (direction, 0, 0)),
                  pl.BlockSpec((1, c), const2),
                  pl.BlockSpec((1, c), const2),
                  pl.BlockSpec(ts.shape, const2),
                  pl.BlockSpec((c, c), const2)],
        out_specs=pl.BlockSpec((None, TB, c), lambda i, t: (i, tmap(t), 0)),
        out_shape=jax.ShapeDtypeStruct((b, tall, c), F32),
        scratch_shapes=[pltpu.VMEM((c // QUAD, QUAD, QUAD), F32),
                        pltpu.VMEM((TB, c), F32)],
        compiler_params=_cp(("parallel", "arbitrary")),
        name="scan_rev" if reverse else "scan_fwd",
    )(r, v, kk, lw, kd, bb, g, rk, gn_g, gn_b, ts, ones_bd)


def _mm_kernel(a_ref, b_ref, o_ref, acc_ref):
    k = pl.program_id(2)

    @pl.when(k == 0)
    def _():
        acc_ref[...] = jnp.zeros_like(acc_ref)

    acc_ref[...] += _dot(a_ref[...], b_ref[...])

    @pl.when(k == pl.num_programs(2) - 1)
    def _():
        o_ref[...] = acc_ref[...]


def _matmul(a, b, tm, tn, tk):
    m, kd = a.shape
    n = b.shape[1]
    tm, tn, tk = min(tm, m), min(tn, n), min(tk, kd)
    return pl.pallas_call(
        _mm_kernel,
        grid=(m // tm, n // tn, kd // tk),
        in_specs=[pl.BlockSpec((tm, tk), lambda i, j, k: (i, k)),
                  pl.BlockSpec((tk, tn), lambda i, j, k: (k, j))],
        out_specs=pl.BlockSpec((tm, tn), lambda i, j, k: (i, j)),
        out_shape=jax.ShapeDtypeStruct((m, n), F32),
        scratch_shapes=[pltpu.VMEM((tm, tn), F32)],
        compiler_params=_cp(("parallel", "parallel", "arbitrary")),
        name="dft",
    )(a, b)


def _dft_matrix(seq):
    hi = seq // 128
    k = np.arange(seq, dtype=np.int64)[:, None]
    a_ang = 2.0 * np.pi * ((k * np.arange(hi)[None, :] * 128) % seq) / seq
    b_ang = 2.0 * np.pi * ((k * np.arange(128)[None, :]) % seq) / seq
    ca = jnp.asarray(np.cos(a_ang), F32)[:, :, None]
    sa = jnp.asarray(np.sin(a_ang), F32)[:, :, None]
    cb = jnp.asarray(np.cos(b_ang), F32)[:, None, :]
    sb = jnp.asarray(np.sin(b_ang), F32)[:, None, :]
    cos = (ca * cb - sa * sb).reshape(seq, seq)
    sin = (sa * cb + ca * sb).reshape(seq, seq)
    return jnp.concatenate([cos, sin], 1).astype(BF16)


def _out_kernel(x_ref, o0_ref, o1_ref, fn_ref, g0_ref, b0_ref, gt_ref, sc_ref, sh_ref, wt_ref, wb_ref, bf_ref,
                g1_ref, b1_ref, wr_ref, br_ref, tri_ref, one_ref,
                x1_o, h2_o, idx_o, gate_o, rank_o, cnt_o):
    xn = _ln(x_ref[...], g0_ref[...], b0_ref[...])
    rw = (o0_ref[...] + o1_ref[...]).astype(BF16)
    fn = (fn_ref[...] + bf_ref[...]).astype(BF16)
    mo = _dot(rw, wt_ref[...]) + _dot(fn, wb_ref[...])
    x1 = _ln(DEEPNORM_ALPHA * xn + gt_ref[...] * mo, g1_ref[...], b1_ref[...])
    h2 = x1 * (1.0 + sc_ref[...]) + sh_ref[...]
    x1_o[...] = x1
    _store_token_rows(h2_o, h2)

    vals = _dot_nt(wr_ref[...], h2, precision=HIGHEST) + br_ref[...]
    ne = vals.shape[0]
    rowid = lax.broadcasted_iota(jnp.int32, vals.shape, 0)
    sels, tops, idxs = [], [], []
    for _ in range(TOP_K):
        mx = jnp.max(vals, axis=0, keepdims=True)
        ix = jnp.min(jnp.where(vals == mx, rowid, ne), axis=0, keepdims=True)
        sel = rowid == ix
        vals = jnp.where(sel, -jnp.inf, vals)
        sels.append(sel)
        tops.append(mx)
        idxs.append(ix)
    ex = [jnp.exp(m - tops[0]) for m in tops]
    den = ex[0] + ex[1] + ex[2] + ex[3]
    gate_o[...] = jnp.concatenate([e / den for e in ex], 0)
    idx_o[...] = jnp.concatenate(idxs, 0)
    onehot = jnp.zeros(vals.shape, F32)
    for sel in sels:
        onehot = onehot + jnp.where(sel, 1.0, 0.0)
    ohb = onehot.astype(BF16)
    before = _dot(ohb, tri_ref[...])
    rank_o[...] = jnp.concatenate(
        [jnp.sum(jnp.where(sel, before, 0.0), axis=0, keepdims=True) for sel in sels], 0).astype(jnp.int32)
    cnt_o[...] = _dot(ohb, one_ref[...])


def _outproj(x, o0, o1, fno, ln0_g, ln0_b, gt1, sc2, sh2, w_top, w_bot, b_fno, ln1_g, ln1_b, w_rt, b_r, tri, ones_col):
    b, seq, d = x.shape
    c = o0.shape[-1]
    nt = seq // TB
    ntile = b * nt
    ne = w_rt.shape[0]
    const2 = lambda i, t: (0, 0)
    vec = pl.BlockSpec((1, d), const2)
    mod = pl.BlockSpec((None, 1, d), lambda i, t: (i, 0, 0))
    tok = lambda i, t: (0, i * nt + t)
    return pl.pallas_call(
        _out_kernel,
        grid=(b, nt),
        in_specs=[pl.BlockSpec((None, TB, d), lambda i, t: (i, t, 0)),
                  pl.BlockSpec((None, TB, c), lambda i, t: (i, t + 1, 0)),
                  pl.BlockSpec((None, TB, c), lambda i, t: (i, t + 1, 0)),
                  pl.BlockSpec((TB, c), lambda i, t: (t, i)),
                  vec, vec, mod, mod, mod,
                  pl.BlockSpec((c, d), const2),
                  pl.BlockSpec((c, d), const2),
                  pl.BlockSpec((1, c), const2),
                  vec, vec,
                  pl.BlockSpec((ne, d), const2),
                  pl.BlockSpec((ne, 1), const2),
                  pl.BlockSpec((TB, TB), const2),
                  pl.BlockSpec((TB, 128), const2)],
        out_specs=[pl.BlockSpec((None, TB, d), lambda i, t: (i, t, 0)),
                   pl.BlockSpec((TB * (d // LANES), LANES), lambda i, t: (i * nt + t, 0)),
                   pl.BlockSpec((TOP_K, TB), tok),
                   pl.BlockSpec((TOP_K, TB), tok),
                   pl.BlockSpec((TOP_K, TB), tok),
                   pl.BlockSpec((None, ne, 128), lambda i, t: (i * nt + t, 0, 0))],
        out_shape=[jax.ShapeDtypeStruct((b, seq, d), F32),
                   jax.ShapeDtypeStruct((b * seq * (d // LANES), LANES), F32),
                   jax.ShapeDtypeStruct((TOP_K, b * seq), jnp.int32),
                   jax.ShapeDtypeStruct((TOP_K, b * seq), F32),
                   jax.ShapeDtypeStruct((TOP_K, b * seq), jnp.int32),
                   jax.ShapeDtypeStruct((ntile, ne, 128), F32)],
        compiler_params=_cp(("parallel", "arbitrary")),
        name="outproj",
    )(x, o0, o1, fno, ln0_g, ln0_b, gt1, sc2, sh2, w_top, w_bot, b_fno, ln1_g, ln1_b, w_rt, b_r, tri, ones_col)


def _slot_kernel(idx_ref, rank_ref, base_ref, o_ref):
    idx = idx_ref[...]
    base = base_ref[...]
    rowid = lax.broadcasted_iota(jnp.int32, (base.shape[0], idx.shape[1]), 0)
    rows = [jnp.sum(jnp.where(rowid == idx[k:k + 1], base, 0), axis=0, keepdims=True) for k in range(TOP_K)]
    o_ref[...] = rank_ref[...] + jnp.concatenate(rows, 0)


def _slots(idx, rank, base):
    ntile, ne, _ = base.shape
    return pl.pallas_call(
        _slot_kernel,
        grid=(ntile,),
        in_specs=[pl.BlockSpec((TOP_K, TB), lambda i: (0, i)),
                  pl.BlockSpec((TOP_K, TB), lambda i: (0, i)),
                  pl.BlockSpec((None, ne, 1), lambda i: (i, 0, 0))],
        out_specs=pl.BlockSpec((None, TOP_K, TB), lambda i: (i, 0, 0)),
        out_shape=jax.ShapeDtypeStruct((ntile, TOP_K, TB), jnp.int32),
        compiler_params=_cp(("parallel",)),
        name="slot",
    )(idx, rank, base)


def _row_copy(src, src_row, dst, dst_row, sem, pieces):
    return pltpu.make_async_copy(src.at[pl.ds(pl.multiple_of(src_row * pieces, pieces), pieces)],
                                 dst.at[pl.ds(pl.multiple_of(dst_row * pieces, pieces), pieces)], sem)


def _slot_table(slot_hbm, slot_smem, ssem):
    i = pl.program_id(0)
    cur = i % 2

    def copy(tile, buf):
        return pltpu.make_async_copy(slot_hbm.at[tile], slot_smem.at[buf], ssem.at[buf])

    @pl.when(i == 0)
    def _():
        copy(0, 0).start()

    @pl.when(i + 1 < pl.num_programs(0))
    def _():
        copy(i + 1, 1 - cur).start()

    copy(i, cur).wait()
    return cur


def _dispatch_kernel(pad_ref, slot_hbm, h_ref, xs_ref, slot_smem, zrow_ref, sem, ssem):
    i = pl.program_id(0)
    pieces = zrow_ref.shape[0]
    cur = _slot_table(slot_hbm, slot_smem, ssem)

    def issue(j, carry):
        for k in range(TOP_K):
            _row_copy(h_ref, j, xs_ref, slot_smem[cur, k, j], sem, pieces).start(priority=k % 2)
        return carry

    lax.fori_loop(0, TB, issue, 0)

    def drain(j, carry):
        for k in range(TOP_K):
            _row_copy(h_ref, 0, xs_ref, 0, sem, pieces).wait()
        return carry

    lax.fori_loop(0, TB, drain, 0)

    @pl.when(i == pl.num_programs(0) - 1)
    def _():
        zrow_ref[...] = jnp.zeros_like(zrow_ref)

        def per_expert(e, carry):
            start = pad_ref[e]
            n = pad_ref[N_EXPERTS + e]

            def zi(q, c2):
                _row_copy(zrow_ref, 0, xs_ref, start + q, sem, pieces).start()
                return c2

            lax.fori_loop(0, n, zi, 0)

            def zw(q, c2):
                _row_copy(zrow_ref, 0, xs_ref, 0, sem, pieces).wait()
                return c2

            lax.fori_loop(0, n, zw, 0)
            return carry

        lax.fori_loop(0, N_EXPERTS, per_expert, 0)


def _dispatch(padinfo, slot, h2, n_slots, pieces):
    ntile = h2.shape[0] // (TB * pieces)
    return pl.pallas_call(
        _dispatch_kernel,
        grid_spec=pltpu.PrefetchScalarGridSpec(
            num_scalar_prefetch=1,
            grid=(ntile,),
            in_specs=[pl.BlockSpec(memory_space=pl.ANY),
                      pl.BlockSpec((TB * pieces, LANES), lambda i, pad: (i, 0))],
            out_specs=pl.BlockSpec(memory_space=pl.ANY),
            scratch_shapes=[pltpu.SMEM((2, TOP_K, TB), jnp.int32),
                            pltpu.VMEM((pieces, LANES), F32),
                            pltpu.SemaphoreType.DMA,
                            pltpu.SemaphoreType.DMA((2,))]),
        out_shape=jax.ShapeDtypeStruct((n_slots * pieces, LANES), F32),
        compiler_params=_cp(("arbitrary",)),
        name="dispatch",
    )(padinfo, slot, h2)


def _expert_kernel(be_ref, nu_ref, x_ref, w1_ref, b1_ref, w2_ref, b2_ref, perm_ref, o_ref, w1b_ref, w2b_ref):
    i = pl.program_id(0)
    prev = be_ref[jnp.maximum(i - 1, 0)]
    changed = jnp.logical_or(i == 0, be_ref[i] != prev)
    dff2 = w1_ref.shape[1]
    nblk = dff2 // 256

    @pl.when(changed)
    def _():
        perm = perm_ref[...]
        for j in range(nblk):
            cs = slice(j * 256, (j + 1) * 256)
            w1b_ref[:, cs] = _dot(w1_ref[:, cs].astype(BF16), perm).astype(BF16)
        w2b_ref[...] = w2_ref[...].astype(BF16)

    @pl.when(i < nu_ref[0])
    def _():
        x = _load_token_rows(x_ref, EBLK, w1_ref.shape[0] // LANES)
        u = _dot(x.astype(BF16), w1b_ref[...]) + b1_ref[...]
        acts = []
        for j in range(nblk):
            glu = jnp.minimum(u[:, j * 256:j * 256 + 128], SWIGLU_LIMIT)
            lin = jnp.clip(u[:, j * 256 + 128:(j + 1) * 256], -SWIGLU_LIMIT, SWIGLU_LIMIT)
            acts.append(glu * _sigmoid(SWIGLU_ALPHA * glu) * (lin + 1.0))
        act = jnp.concatenate(acts, 1).astype(BF16)
        _store_token_rows(o_ref, _dot(act, w2b_ref[...]) + b2_ref[...])

    @pl.when(i >= nu_ref[0])
    def _():
        o_ref[...] = jnp.zeros_like(o_ref)


def _experts(block_e, n_used, xs, w1, b1p, w2, b2, perm):
    ne, d, dff2 = w1.shape
    dff = w2.shape[1]
    pieces = d // LANES
    nblocks = xs.shape[0] // (EBLK * pieces)
    return pl.pallas_call(
        _expert_kernel,
        grid_spec=pltpu.PrefetchScalarGridSpec(
            num_scalar_prefetch=2,
            grid=(nblocks,),
            in_specs=[pl.BlockSpec((EBLK * pieces, LANES), lambda i, be, nu: (jnp.minimum(i, nu[0] - 1), 0)),
                      pl.BlockSpec((None, d, dff2), lambda i, be, nu: (be[i], 0, 0)),
                      pl.BlockSpec((None, 1, dff2), lambda i, be, nu: (be[i], 0, 0)),
                      pl.BlockSpec((None, dff, d), lambda i, be, nu: (be[i], 0, 0)),
                      pl.BlockSpec((None, 1, d), lambda i, be, nu: (be[i], 0, 0)),
                      pl.BlockSpec((256, 256), lambda i, be, nu: (0, 0))],
            out_specs=pl.BlockSpec((EBLK * pieces, LANES), lambda i, be, nu: (i, 0)),
            scratch_shapes=[pltpu.VMEM((d, dff2), BF16),
                            pltpu.VMEM((dff, d), BF16)]),
        out_shape=jax.ShapeDtypeStruct(xs.shape, F32),
        compiler_params=_cp(("arbitrary",)),
        name="experts",
    )(block_e, n_used, xs, w1, b1p, w2, b2, perm)


def _combine_kernel(slot_hbm, ys_ref, gate_ref, x1_ref, gt_ref, g_ref, b_ref, eye_ref, o_ref, slot_smem, buf_ref, sem, ssem):
    i = pl.program_id(0)
    pieces = o_ref.shape[1] // LANES
    cur = _slot_table(slot_hbm, slot_smem, ssem)

    def issue(j, carry):
        for k in range(TOP_K):
            _row_copy(ys_ref, slot_smem[cur, k, j], buf_ref.at[k], j, sem, pieces).start(priority=k % 2)
        return carry

    lax.fori_loop(0, TB, issue, 0)

    def drain(j, carry):
        for k in range(TOP_K):
            _row_copy(ys_ref, 0, buf_ref.at[k], 0, sem, pieces).wait()
        return carry

    lax.fori_loop(0, TB, drain, 0)

    gt = _dot_nt(eye_ref[...], gate_ref[...], precision=HIGHEST)
    f = gt[:, 0:1] * _load_token_rows(buf_ref.at[0], TB, pieces)
    for k in range(1, TOP_K):
        f = f + gt[:, k:k + 1] * _load_token_rows(buf_ref.at[k], TB, pieces)
    o_ref[...] = _ln(DEEPNORM_ALPHA * x1_ref[...] + gt_ref[...] * f, g_ref[...], b_ref[...])


def _combine(slot, ys, gates, x1, gt2, ln_g, ln_b, eye, tiles_per_batch):
    n, d = x1.shape
    ntile = n // TB
    return pl.pallas_call(
        _combine_kernel,
        grid=(ntile,),
        in_specs=[pl.BlockSpec(memory_space=pl.ANY),
                  pl.BlockSpec(memory_space=pl.ANY),
                  pl.BlockSpec((TOP_K, TB), lambda i: (0, i)),
                  pl.BlockSpec((TB, d), lambda i: (i, 0)),
                  pl.BlockSpec((None, 1, d), lambda i: (i // tiles_per_batch, 0, 0)),
                  pl.BlockSpec((1, d), lambda i: (0, 0)),
                  pl.BlockSpec((1, d), lambda i: (0, 0)),
                  pl.BlockSpec((TB, TB), lambda i: (0, 0))],
        out_specs=pl.BlockSpec((TB, d), lambda i: (i, 0)),
        out_shape=jax.ShapeDtypeStruct((n, d), F32),
        scratch_shapes=[pltpu.SMEM((2, TOP_K, TB), jnp.int32),
                        pltpu.VMEM((TOP_K, TB * (d // LANES), LANES), F32),
                        pltpu.SemaphoreType.DMA,
                        pltpu.SemaphoreType.DMA((2,))],
        compiler_params=_cp(("arbitrary",)),
        name="combine",
    )(slot, ys, gates, x1, gt2, ln_g, ln_b, eye)


def _scan_tables(reverse):
    t = np.arange(TB)
    same = (t[:, None] // CHUNK) == (t[None, :] // CHUNK)
    tri = (t[None, :] >= t[:, None]) if reverse else (t[None, :] <= t[:, None])
    return jnp.asarray(np.concatenate([same & tri, same], 0).astype(np.float32), BF16)


def _head_ones(c):
    h = np.arange(c) // HEAD
    return jnp.asarray((h[:, None] == h[None, :]).astype(np.float32), BF16)


def _deinterleave_perm():
    p = np.zeros((256, 256), np.float32)
    j = np.arange(128)
    p[2 * j, j] = 1.0
    p[2 * j + 1, 128 + j] = 1.0
    return jnp.asarray(p, BF16)


def kernel(x, c, ctx, c_ctx, ln0_g, ln0_b, w_ada, b_ada, w_in, mu_shift, w0, w2_decay, a0, a2_iclr, g2_gate, r_k, k_k, k_a, gn_g, gn_b, w_fno, b_fno, w_out, ln1_g, ln1_b, w_router, b_router, w1, b1, w2, b2, ln2_g, ln2_b):
    b, seq, d = x.shape
    ctx_len = ctx.shape[1]
    assert ctx_len == TB and seq % TB == 0 and w_ada.shape[0] == 1
    n_dir, cw = w0.shape[1], w0.shape[2]
    fgroups, gw = w_fno.shape[1], w_fno.shape[2]
    fw = fgroups * gw
    shift_w = mu_shift.shape[1]
    nd, na, ng = w2_decay.shape[2], a2_iclr.shape[2], g2_gate.shape[2]
    assert n_dir == 2 and shift_w == 3 * cw + 2 * (nd + na + ng)
    ne = w_router.shape[2]
    n_tok = b * seq
    row = lambda a: a.reshape(1, -1)

    rows = -(-(b + 1) // 8) * 8
    cc = jnp.zeros((rows, d), F32).at[:b].set(c).at[b].set(c_ctx)
    mod = _ada(cc, w_ada[0], b_ada[0])
    sh1, sc1, gt1, sh2, sc2, gt2 = [mod[:b, i * d:(i + 1) * d] for i in range(6)]
    sh1c, sc1c = mod[b, :d], mod[b, d:2 * d]
    scsel = jnp.stack([jnp.broadcast_to(sc1c, (b, d)), sc1], 1)[:, :, None, :]
    shsel = jnp.stack([jnp.broadcast_to(sh1c, (b, d)), sh1], 1)[:, :, None, :]
    mod3 = lambda a: a[:, None, :]

    cidx = np.arange(gw)
    ang = 2.0 * np.pi * ((cidx[:, None] * cidx[None, :]) % gw) / gw
    norm = 1.0 / math.sqrt(seq * gw)
    csc = jnp.asarray(np.stack([np.cos(ang) * norm, -np.sin(ang) * norm]), F32)
    win_f3 = w_in[0][:, shift_w:].reshape(d, fgroups, gw).transpose(1, 0, 2)
    wfc, wfs = _fold(csc, w_fno[0], win_f3)
    unf = lambda a: a.transpose(1, 0, 2).reshape(d, fw)
    w3 = jnp.concatenate([w_in[0][:, :shift_w], unf(wfc), unf(wfs)], 1).astype(BF16)

    s_all, fcat = _inproj(ctx, x, scsel, shsel, row(ln0_g), row(ln0_b), w3, shift_w, fw)

    def pad_dir(w, width):
        out = jnp.zeros((2, 2 * width, cw), F32)
        return out.at[0, :width].set(w[0]).at[1, width:].set(w[1]).astype(BF16)

    ones_bd = _head_ones(cw)
    r, v, kk, lw, kd, bb, g = _prep(
        s_all, row(mu_shift[0]), w0[0][:, None, :], pad_dir(w2_decay[0], nd), a0[0][:, None, :],
        pad_dir(a2_iclr[0], na), pad_dir(g2_gate[0], ng), row(k_k[0]), row(k_a[0]), ones_bd, cw)

    rk = r_k[0].reshape(2, 1, cw)
    outs = [_scan(r, v, kk, lw, kd, bb, g, rk, row(gn_g[0]), row(gn_b[0]), _scan_tables(dr == 1), ones_bd, dr)
            for dr in range(2)]

    fno = _matmul(_dft_matrix(seq), fcat.reshape(2 * seq, b * fw), 1024, 2048, 1024)

    ids = np.arange(TB)
    tri = jnp.asarray((ids[:, None] < ids[None, :]).astype(np.float32), BF16)
    ones_col = jnp.ones((TB, 128), BF16)
    wo = w_out[0].astype(BF16)
    x1, h2, idx, gates, rank, cnt = _outproj(
        x, outs[0], outs[1], fno, row(ln0_g), row(ln0_b), mod3(gt1), mod3(sc2), mod3(sh2), wo[:cw], wo[cw:],
        row(b_fno[0]), row(ln1_g[0]), row(ln1_b[0]), w_router[0].T, b_router[0].reshape(ne, 1), tri, ones_col)

    ntile = n_tok // TB
    cnt_t = cnt[:, :, 0].astype(jnp.int32)
    counts = jnp.sum(cnt_t, 0)
    padded = (counts + EBLK - 1) // EBLK * EBLK
    pends = jnp.cumsum(padded)
    pstarts = pends - padded
    base = (pstarts[None, :] + jnp.cumsum(cnt_t, 0) - cnt_t)[:, :, None]
    nblocks = (n_tok * TOP_K) // EBLK + ne
    n_slots = nblocks * EBLK
    block_start = jnp.arange(nblocks, dtype=jnp.int32) * EBLK
    block_e = jnp.minimum(jnp.sum((pends[None, :] <= block_start[:, None]).astype(jnp.int32), 1), ne - 1)
    n_used = (pends[-1:] // EBLK).astype(jnp.int32)
    padinfo = jnp.concatenate([pstarts + counts, padded - counts]).astype(jnp.int32)

    slot = _slots(idx, rank, base)
    xs = _dispatch(padinfo, slot, h2, n_slots, d // LANES)
    b1p = b1[0].reshape(ne, -1, 128, 2).transpose(0, 1, 3, 2).reshape(ne, 1, -1)
    ys = _experts(block_e, n_used, xs, w1[0], b1p, w2[0], b2[0][:, None, :], _deinterleave_perm())
    out = _combine(slot, ys, gates, x1.reshape(n_tok, d), mod3(gt2), row(ln2_g[0]), row(ln2_b[0]),
                   jnp.eye(TB, dtype=F32), seq // TB)
    return out.reshape(b, seq, d)
```

```python
import functools
import math

import numpy as np
import jax
import jax.numpy as jnp
from jax import lax
from jax.experimental import pallas as pl
from jax.experimental.pallas import tpu as pltpu

F32 = jnp.float32
BF16 = jnp.bfloat16
HIGHEST = lax.Precision.HIGHEST

GRID_W = 64
HEAD = 64
CHUNK = 64
TB = 256
N_EXPERTS = 32
TOP_K = 4
EBLK = 512
SWIGLU_LIMIT = 7.0
SWIGLU_ALPHA = 1.702
LN_EPS = 1e-5
GN_EPS = 64e-5
DEEPNORM_ALPHA = 2.0 ** 0.25
DECAY_SCALE = math.exp(-0.5)
LANES = 128
VMEM_LIMIT = 56 * 1024 * 1024


def _cp(sem, vmem=VMEM_LIMIT):
    return pltpu.CompilerParams(dimension_semantics=sem, vmem_limit_bytes=vmem)


def _sigmoid(x):
    return 1.0 / (1.0 + jnp.exp(-x))


def _ln(u, g, b):
    mean = jnp.mean(u, -1, keepdims=True)
    d = u - mean
    var = jnp.mean(d * d, -1, keepdims=True)
    return d * lax.rsqrt(var + LN_EPS) * g + b


def _dot(a, b):
    return jnp.dot(a, b, preferred_element_type=F32)


def _dot_nt(a, b, precision=None):
    return lax.dot_general(a, b, (((1,), (1,)), ((), ())), precision=precision, preferred_element_type=F32)


def _split_dot(x, w):
    hi = x.astype(BF16)
    lo = (x - hi.astype(F32)).astype(BF16)
    return _dot(hi, w) + _dot(lo, w)


def _load_token_rows(ref, n, pieces):
    return jnp.concatenate([ref[pl.ds(j, n, stride=pieces), :] for j in range(pieces)], 1)


def _store_token_rows(ref, val):
    n, width = val.shape
    pieces = width // LANES
    for j in range(pieces):
        ref[pl.ds(j, n, stride=pieces), :] = val[:, j * LANES:(j + 1) * LANES]


def _ada_kernel(c_ref, w_ref, b_ref, o_ref):
    c = c_ref[...]
    s = c * _sigmoid(c)
    o_ref[...] = jnp.dot(s, w_ref[...], precision=HIGHEST, preferred_element_type=F32) + b_ref[...]


def _ada(cc, w_ada, b_ada):
    rows, d = cc.shape
    n = w_ada.shape[1]
    tn = 1024
    return pl.pallas_call(
        _ada_kernel,
        grid=(n // tn,),
        in_specs=[pl.BlockSpec((rows, d), lambda j: (0, 0)),
                  pl.BlockSpec((d, tn), lambda j: (0, j)),
                  pl.BlockSpec((1, tn), lambda j: (0, j))],
        out_specs=pl.BlockSpec((rows, tn), lambda j: (0, j)),
        out_shape=jax.ShapeDtypeStruct((rows, n), F32),
        compiler_params=_cp(("parallel",)),
        name="ada",
    )(cc, w_ada, b_ada.reshape(1, n))


def _fold_kernel(cs_ref, wf_ref, win_ref, oc_ref, os_ref):
    wf = wf_ref[...]
    mc = jnp.dot(cs_ref[0], wf, precision=HIGHEST, preferred_element_type=F32)
    ms = jnp.dot(cs_ref[1], wf, precision=HIGHEST, preferred_element_type=F32)
    w = win_ref[...]
    oc_ref[...] = jnp.dot(w, mc, precision=HIGHEST, preferred_element_type=F32)
    os_ref[...] = jnp.dot(w, ms, precision=HIGHEST, preferred_element_type=F32)


def _fold(csc, w_fno, win_f3):
    g, d, gw = win_f3.shape
    return pl.pallas_call(
        _fold_kernel,
        grid=(g,),
        in_specs=[pl.BlockSpec((2, gw, gw), lambda i: (0, 0, 0)),
                  pl.BlockSpec((None, gw, gw), lambda i: (i, 0, 0)),
                  pl.BlockSpec((None, d, gw), lambda i: (i, 0, 0))],
        out_specs=[pl.BlockSpec((None, d, gw), lambda i: (i, 0, 0)),
                   pl.BlockSpec((None, d, gw), lambda i: (i, 0, 0))],
        out_shape=[jax.ShapeDtypeStruct((g, d, gw), F32)] * 2,
        compiler_params=_cp(("parallel",)),
        name="fold",
    )(csc, w_fno, win_f3)


def _in_kernel(ctx_ref, x_ref, sc_ref, sh_ref, g_ref, b_ref, w_ref, s_ref, f_ref, *, shift_w, fw):
    xin = jnp.where(pl.program_id(1) == 0, ctx_ref[...], x_ref[...])
    xn = _ln(xin, g_ref[...], b_ref[...])
    h = xn * (1.0 + sc_ref[...]) + sh_ref[...]
    p = _dot(h.astype(BF16), w_ref[...])
    s_ref[...] = p[:, :shift_w]
    f_ref[0] = p[:, shift_w:shift_w + fw].astype(BF16)
    f_ref[1] = p[:, shift_w + fw:].astype(BF16)


def _inproj(ctx, x, scsel, shsel, ln_g, ln_b, w3, shift_w, fw):
    b, seq, d = x.shape
    tall = ctx.shape[1] + seq
    nt = tall // TB
    wn = w3.shape[1]
    return pl.pallas_call(
        functools.partial(_in_kernel, shift_w=shift_w, fw=fw),
        grid=(b, nt),
        in_specs=[pl.BlockSpec((None, TB, d), lambda i, t: (i, 0, 0)),
                  pl.BlockSpec((None, TB, d), lambda i, t: (i, jnp.maximum(t - 1, 0), 0)),
                  pl.BlockSpec((None, None, 1, d), lambda i, t: (i, jnp.minimum(t, 1), 0, 0)),
                  pl.BlockSpec((None, None, 1, d), lambda i, t: (i, jnp.minimum(t, 1), 0, 0)),
                  pl.BlockSpec((1, d), lambda i, t: (0, 0)),
                  pl.BlockSpec((1, d), lambda i, t: (0, 0)),
                  pl.BlockSpec((d, wn), lambda i, t: (0, 0))],
        out_specs=[pl.BlockSpec((None, TB, shift_w), lambda i, t: (i, t, 0)),
                   pl.BlockSpec((2, TB, fw), lambda i, t: (0, jnp.maximum(t - 1, 0), i))],
        out_shape=[jax.ShapeDtypeStruct((b, tall, shift_w), F32),
                   jax.ShapeDtypeStruct((2, seq, b * fw), BF16)],
        compiler_params=_cp(("parallel", "arbitrary")),
        name="inproj",
    )(ctx, x, scsel, shsel, ln_g, ln_b, w3)


def _prep_kernel(s_ref, sp_ref, sn_ref, mu_ref, w0_ref, w2d_ref, a0_ref, a2_ref, g2_ref, kkw_ref, ka_ref, ones_ref,
                 r_o, v_o, kk_o, lw_o, kd_o, bb_o, g_o, *, c):
    t = pl.program_id(1)
    nt = pl.num_programs(1)
    s = s_ref[...]
    idx = lax.broadcasted_iota(jnp.int32, (TB, 1), 0)
    col = idx & (GRID_W - 1)
    is_ctx = t == 0
    lmask = jnp.where(is_ctx, idx, col) == 0
    rmask = jnp.where(is_ctx, idx - (TB - 1), col - (GRID_W - 1)) == 0
    left = jnp.where(lmask, 0.0, pltpu.roll(s, 1, 0))
    right = jnp.where(rmask, 0.0, pltpu.roll(s, TB - 1, 0))
    up = jnp.concatenate([jnp.where(t == 1, 0.0, sp_ref[...]), s[:TB - GRID_W]], 0)
    down = jnp.concatenate([s[GRID_W:], jnp.where(t == nt - 1, 0.0, sn_ref[...])], 0)
    ud = jnp.where(is_ctx, 0.0, up + down)
    sh = (ud + left + right) * jnp.where(is_ctx, 0.5, 0.25)
    m = s + mu_ref[...] * (sh - s)

    r = m[:, :c]
    k = m[:, c:2 * c]
    v = m[:, 2 * c:3 * c]
    o = 3 * c
    nd = w2d_ref.shape[1]
    na = a2_ref.shape[1]
    ng = g2_ref.shape[1]
    wd = jnp.tanh(m[:, o:o + nd]).astype(BF16)
    ad = m[:, o + nd:o + nd + na].astype(BF16)
    gd = _sigmoid(m[:, o + nd + na:o + nd + na + ng]).astype(BF16)
    kk = k * kkw_ref[...]
    ss = _split_dot(kk * kk, ones_ref[...])
    kk = kk / jnp.maximum(jnp.sqrt(ss), 1e-12)
    r_o[...] = r
    v_o[...] = v
    kk_o[...] = kk
    for d in range(2):
        wl = w0_ref[d] + _dot(wd, w2d_ref[d])
        lw_o[d] = -DECAY_SCALE * _sigmoid(wl)
        a = _sigmoid(a0_ref[d] + _dot(ad, a2_ref[d]))
        g_o[d] = _dot(gd, g2_ref[d])
        kd_o[d] = k * (1.0 + (a - 1.0) * ka_ref[...])
        bb_o[d] = kk * a


def _prep(s_all, mu, w0, w2d, a0, a2, g2, k_k, k_a, ones_bd, c):
    b, tall, sw = s_all.shape
    nt = tall // TB
    nhb = tall // GRID_W
    hb = TB // GRID_W
    const2 = lambda i, t: (0, 0)
    const3 = lambda i, t: (0, 0, 0)
    o1 = pl.BlockSpec((None, TB, c), lambda i, t: (i, t, 0))
    o2 = pl.BlockSpec((2, None, TB, c), lambda i, t: (0, i, t, 0))
    s1 = jax.ShapeDtypeStruct((b, tall, c), F32)
    s2 = jax.ShapeDtypeStruct((2, b, tall, c), F32)
    return pl.pallas_call(
        functools.partial(_prep_kernel, c=c),
        grid=(b, nt),
        in_specs=[pl.BlockSpec((None, TB, sw), lambda i, t: (i, t, 0)),
                  pl.BlockSpec((None, GRID_W, sw), lambda i, t: (i, jnp.maximum(t * hb - 1, 0), 0)),
                  pl.BlockSpec((None, GRID_W, sw), lambda i, t: (i, jnp.minimum(t * hb + hb, nhb - 1), 0)),
                  pl.BlockSpec((1, sw), const2),
                  pl.BlockSpec(w0.shape, const3),
                  pl.BlockSpec(w2d.shape, const3),
                  pl.BlockSpec(a0.shape, const3),
                  pl.BlockSpec(a2.shape, const3),
                  pl.BlockSpec(g2.shape, const3),
                  pl.BlockSpec((1, c), const2),
                  pl.BlockSpec((1, c), const2),
                  pl.BlockSpec((c, c), const2)],
        out_specs=[o1, o1, o1, o2, o2, o2, o2],
        out_shape=[s1, s1, s1, s2, s2, s2, s2],
        compiler_params=_cp(("parallel", "arbitrary")),
        name="prep",
    )(s_all, s_all, s_all, mu, w0, w2d, a0, a2, g2, k_k, k_a, ones_bd)


QUAD = 4 * HEAD


def _unit_triangular_inverses(ns):
    ti = lax.broadcasted_iota(jnp.int32, ns[0].shape, 0)
    tj = lax.broadcasted_iota(jnp.int32, ns[0].shape, 1)

    def same_block(shift):
        return (ti >> shift) == (tj >> shift)

    zero = jnp.zeros_like(ns[0])
    n8 = [jnp.where(same_block(3), n, zero) for n in ns]
    n8s = [_dot(a, a).astype(BF16) for a in n8]
    n8q = [_dot(a, a).astype(BF16) for a in n8s]
    xs = [jnp.where(ti == tj, jnp.ones_like(a), a) for a in n8]
    xs = [(x.astype(F32) + _dot(x, a)).astype(BF16) for x, a in zip(xs, n8s)]
    xs = [(x.astype(F32) + _dot(x, a)).astype(BF16) for x, a in zip(xs, n8q)]
    for shift in (3, 4, 5):
        offs = [jnp.where(same_block(shift + 1), jnp.where(same_block(shift), zero, n), zero) for n in ns]
        fs = [_dot(x, off).astype(BF16) for x, off in zip(xs, offs)]
        xs = [(x.astype(F32) + _dot(f, x)).astype(BF16) for x, f in zip(xs, fs)]
    return xs


def _scan_kernel(r_ref, v_ref, kk_ref, lw_ref, kd_ref, bb_ref, g_ref, rk_ref, gng_ref, gnb_ref, ts_ref, ones_ref,
                 o_ref, st_ref, y_ref, *, reverse):
    t = pl.program_id(1)
    width = r_ref.shape[-1]
    nquad = width // QUAD
    nchunk = TB // CHUNK
    nh = QUAD // HEAD

    @pl.when(t == 0)
    def _():
        st_ref[...] = jnp.zeros_like(st_ref)

    r = r_ref[...]
    v = v_ref[...]
    kk = kk_ref[...]
    lw = lw_ref[...]
    kd = kd_ref[...]
    bb = bb_ref[...]

    p1 = lw.astype(BF16)
    r1 = lw - p1.astype(F32)
    p2 = r1.astype(BF16)
    p3 = (r1 - p2.astype(F32)).astype(BF16)
    ts = ts_ref[...]
    acc = _dot(ts, p1) + _dot(ts, p2) + _dot(ts, p3)
    cl = acc[:TB]
    tot = acc[TB:]
    dec_in = jnp.exp(cl)
    dec_inv = jnp.exp(-cl)
    dec_ex = jnp.exp(cl - lw)
    dec_end = jnp.exp(tot - cl)
    dec_all = jnp.exp(tot)
    at = (-(kk * dec_ex)).astype(BF16)
    bt = (bb * dec_inv).astype(BF16)
    kt = (kd * dec_inv).astype(BF16)
    rt = (r * dec_in).astype(BF16)
    bh = (bb * dec_end).astype(BF16)
    kh = (kd * dec_end).astype(BF16)
    vb = v.astype(BF16)

    lane_head = lax.broadcasted_iota(jnp.int32, (CHUNK, QUAD), 1) // HEAD
    qi = lax.broadcasted_iota(jnp.int32, (QUAD, QUAD), 0)
    qj = lax.broadcasted_iota(jnp.int32, (QUAD, QUAD), 1)
    ti = qi & (CHUNK - 1)
    sj = qj & (CHUNK - 1)
    strict = (sj > ti) if reverse else (sj < ti)
    incl = (sj >= ti) if reverse else (sj <= ti)
    same_head = (qi // HEAD) == (qj // HEAD)

    def stack_heads(x):
        return jnp.concatenate([jnp.where(lane_head == h, x, jnp.zeros_like(x)) for h in range(nh)], 0)

    def fold_heads(x):
        out = x[:CHUNK]
        for h in range(1, nh):
            out = out + x[h * CHUNK:(h + 1) * CHUNK]
        return out

    units = [(c, q) for c in range(nchunk) for q in range(nquad)]
    pre = {}
    for c, q in units:
        rs = slice(c * CHUNK, (c + 1) * CHUNK)
        ls = slice(q * QUAD, (q + 1) * QUAD)
        a_st = stack_heads(at[rs, ls])
        r_st = stack_heads(rt[rs, ls])
        b_st = stack_heads(bt[rs, ls])
        k_st = stack_heads(kt[rs, ls])
        v_st = stack_heads(vb[rs, ls])
        qq = _dot_nt(jnp.concatenate([a_st, r_st], 0), jnp.concatenate([b_st, k_st], 0))
        n_ab = jnp.where(strict, qq[:QUAD, :QUAD], 0.0).astype(BF16)
        l_ak = jnp.where(strict, qq[:QUAD, QUAD:], 0.0).astype(BF16)
        m_r = jnp.concatenate([jnp.where(incl, qq[QUAD:, :QUAD], 0.0),
                               jnp.where(incl, qq[QUAD:, QUAD:], 0.0)], 1).astype(BF16)
        pre[(c, q)] = (a_st, v_st, n_ab, l_ak, m_r)
    inv = _unit_triangular_inverses([pre[u][2] for u in units])
    zq = jnp.zeros((CHUNK, QUAD), BF16)
    zf = jnp.zeros((CHUNK, QUAD), F32)
    rows = [slice(c * CHUNK, (c + 1) * CHUNK) for c, q in units]
    lanes = [slice(q * QUAD, (q + 1) * QUAD) for c, q in units]
    a_sts, v_sts, _, l_aks, m_rs = zip(*[pre[u] for u in units])
    lvs = [_dot(l_ak, v_st).astype(BF16) for l_ak, v_st in zip(l_aks, v_sts)]
    xws = [_dot(x, jnp.concatenate([a_st, lv], 1)) for x, a_st, lv in zip(inv, a_sts, lvs)]
    w_sts = [xw[:, :QUAD] for xw in xws]
    u0_sts = [xw[:, QUAD:] for xw in xws]
    rhats = [rt[rs, ls].astype(F32) + fold_heads(_dot(m_r[:, :QUAD], w_st.astype(BF16)))
             for rs, ls, m_r, w_st in zip(rows, lanes, m_rs, w_sts)]
    ycs = [fold_heads(_dot(m_r, jnp.concatenate([u0_st.astype(BF16), v_st], 0)))
           for m_r, u0_st, v_st in zip(m_rs, u0_sts, v_sts)]
    wuvs = [jnp.concatenate([fold_heads(w_st), fold_heads(u0_st), v[rs, ls], zf], 0).T.astype(BF16)
            for rs, ls, w_st, u0_st in zip(rows, lanes, w_sts, u0_sts)]
    ends = [jnp.concatenate([jnp.concatenate([bh[rs, ls], zq], 1), jnp.concatenate([zq, bh[rs, ls]], 1),
                             jnp.concatenate([zq, kh[rs, ls]], 1), jnp.concatenate([zq, zq], 1)], 0)
            for rs, ls in zip(rows, lanes)]
    ghs = [_dot(wuv, end) for wuv, end in zip(wuvs, ends)]
    fin = {}
    for u, rhat, yc, gh in zip(units, rhats, ycs, ghs):
        g = jnp.where(same_head, gh[:, :QUAD], 0.0).astype(BF16)
        hc = jnp.where(same_head, gh[:, QUAD:], 0.0)
        fin[u] = (rhat.astype(BF16), yc, g, hc)

    order = range(nchunk - 1, -1, -1) if reverse else range(nchunk)
    states = [st_ref[q] for q in range(nquad)]
    for c in order:
        rs = slice(c * CHUNK, (c + 1) * CHUNK)
        for q in range(nquad):
            ls = slice(q * QUAD, (q + 1) * QUAD)
            rhat, yc, g, hc = fin[(c, q)]
            st = states[q]
            stb = st.astype(BF16)
            y_ref[rs, ls] = _dot_nt(rhat, stb) + yc
            states[q] = st * dec_all[c * CHUNK:c * CHUNK + 1, ls] + _dot(stb, g) + hc
    for q in range(nquad):
        st_ref[q] = states[q]

    ones = ones_ref[:QUAD, :QUAD]

    def head_sum(x):
        return jnp.concatenate([_split_dot(x[:, q * QUAD:(q + 1) * QUAD], ones) for q in range(nquad)], 1)

    inv_n = 1.0 / HEAD
    y = y_ref[...]
    mean = head_sum(y) * inv_n
    d = y - mean
    var = head_sum(d * d) * inv_n
    yn = d * lax.rsqrt(var + GN_EPS) * gng_ref[...] + gnb_ref[...]
    bonus = head_sum(r * kd * rk_ref[...]) * v
    o_ref[...] = (yn + bonus) * g_ref[...]


def _scan(r, v, kk, lw, kd, bb, g, rk, gn_g, gn_b, ts, ones_bd, direction):
    b, tall, c = r.shape
    nt = tall // TB
    reverse = direction == 1
    if reverse:
        tmap = lambda t: jnp.where(t == 0, 0, nt - t)
    else:
        tmap = lambda t: t
    shared = pl.BlockSpec((None, TB, c), lambda i, t: (i, tmap(t), 0))
    perdir = pl.BlockSpec((None, None, TB, c), lambda i, t: (direction, i, tmap(t), 0))
    const2 = lambda i, t: (0, 0)
    return pl.pallas_call(
        functools.partial(_scan_kernel, reverse=reverse),
        grid=(b, nt),
        in_specs=[shared, shared, shared, perdir, perdir, perdir, perdir,
                  pl.BlockSpec((None, 1, c), lambda i, t: (direction, 0, 0)),
                  pl.BlockSpec((1, c), const2),
                  pl.BlockSpec((1, c), const2),
                  pl.BlockSpec(ts.shape, const2),
                  pl.BlockSpec((c, c), const2)],
        out_specs=pl.BlockSpec((None, TB, c), lambda i, t: (i, tmap(t), 0)),
        out_shape=jax.ShapeDtypeStruct((b, tall, c), F32),
        scratch_shapes=[pltpu.VMEM((c // QUAD, QUAD, QUAD), F32),
                        pltpu.VMEM((TB, c), F32)],
        compiler_params=_cp(("parallel", "arbitrary")),
        name="scan_rev" if reverse else "scan_fwd",
    )(r, v, kk, lw, kd, bb, g, rk, gn_g, gn_b, ts, ones_bd)


def _mm_kernel(a_ref, b_ref, o_ref, acc_ref):
    k = pl.program_id(2)

    @pl.when(k == 0)
    def _():
        acc_ref[...] = jnp.zeros_like(acc_ref)

    acc_ref[...] += _dot(a_ref[...], b_ref[...])

    @pl.when(k == pl.num_programs(2) - 1)
    def _():
        o_ref[...] = acc_ref[...]


def _matmul(a, b, tm, tn, tk):
    m, kd = a.shape
    n = b.shape[1]
    tm, tn, tk = min(tm, m), min(tn, n), min(tk, kd)
    return pl.pallas_call(
        _mm_kernel,
        grid=(m // tm, n // tn, kd // tk),
        in_specs=[pl.BlockSpec((tm, tk), lambda i, j, k: (i, k)),
                  pl.BlockSpec((tk, tn), lambda i, j, k: (k, j))],
        out_specs=pl.BlockSpec((tm, tn), lambda i, j, k: (i, j)),
        out_shape=jax.ShapeDtypeStruct((m, n), F32),
        scratch_shapes=[pltpu.VMEM((tm, tn), F32)],
        compiler_params=_cp(("parallel", "parallel", "arbitrary")),
        name="dft",
    )(a, b)


def _dft_matrix(seq):
    hi = seq // 128
    k = np.arange(seq, dtype=np.int64)[:, None]
    a_ang = 2.0 * np.pi * ((k * np.arange(hi)[None, :] * 128) % seq) / seq
    b_ang = 2.0 * np.pi * ((k * np.arange(128)[None, :]) % seq) / seq
    ca = jnp.asarray(np.cos(a_ang), F32)[:, :, None]
    sa = jnp.asarray(np.sin(a_ang), F32)[:, :, None]
    cb = jnp.asarray(np.cos(b_ang), F32)[:, None, :]
    sb = jnp.asarray(np.sin(b_ang), F32)[:, None, :]
    cos = (ca * cb - sa * sb).reshape(seq, seq)
    sin = (sa * cb + ca * sb).reshape(seq, seq)
    return jnp.concatenate([cos, sin], 1).astype(BF16)


def _out_kernel(x_ref, o0_ref, o1_ref, fn_ref, g0_ref, b0_ref, gt_ref, sc_ref, sh_ref, wt_ref, wb_ref, bf_ref,
                g1_ref, b1_ref, wr_ref, br_ref, tri_ref, one_ref,
                x1_o, h2_o, idx_o, gate_o, rank_o, cnt_o):
    xn = _ln(x_ref[...], g0_ref[...], b0_ref[...])
    rw = (o0_ref[...] + o1_ref[...]).astype(BF16)
    fn = (fn_ref[...] + bf_ref[...]).astype(BF16)
    mo = _dot(rw, wt_ref[...]) + _dot(fn, wb_ref[...])
    x1 = _ln(DEEPNORM_ALPHA * xn + gt_ref[...] * mo, g1_ref[...], b1_ref[...])
    h2 = x1 * (1.0 + sc_ref[...]) + sh_ref[...]
    x1_o[...] = x1
    _store_token_rows(h2_o, h2)

    vals = _dot_nt(wr_ref[...], h2, precision=HIGHEST) + br_ref[...]
    ne = vals.shape[0]
    rowid = lax.broadcasted_iota(jnp.int32, vals.shape, 0)
    sels, tops, idxs = [], [], []
    for _ in range(TOP_K):
        mx = jnp.max(vals, axis=0, keepdims=True)
        ix = jnp.min(jnp.where(vals == mx, rowid, ne), axis=0, keepdims=True)
        sel = rowid == ix
        vals = jnp.where(sel, -jnp.inf, vals)
        sels.append(sel)
        tops.append(mx)
        idxs.append(ix)
    ex = [jnp.exp(m - tops[0]) for m in tops]
    den = ex[0] + ex[1] + ex[2] + ex[3]
    gate_o[...] = jnp.concatenate([e / den for e in ex], 0)
    idx_o[...] = jnp.concatenate(idxs, 0)
    onehot = jnp.zeros(vals.shape, F32)
    for sel in sels:
        onehot = onehot + jnp.where(sel, 1.0, 0.0)
    ohb = onehot.astype(BF16)
    before = _dot(ohb, tri_ref[...])
    rank_o[...] = jnp.concatenate(
        [jnp.sum(jnp.where(sel, before, 0.0), axis=0, keepdims=True) for sel in sels], 0).astype(jnp.int32)
    cnt_o[...] = _dot(ohb, one_ref[...])


def _outproj(x, o0, o1, fno, ln0_g, ln0_b, gt1, sc2, sh2, w_top, w_bot, b_fno, ln1_g, ln1_b, w_rt, b_r, tri, ones_col):
    b, seq, d = x.shape
    c = o0.shape[-1]
    nt = seq // TB
    ntile = b * nt
    ne = w_rt.shape[0]
    const2 = lambda i, t: (0, 0)
    vec = pl.BlockSpec((1, d), const2)
    mod = pl.BlockSpec((None, 1, d), lambda i, t: (i, 0, 0))
    tok = lambda i, t: (0, i * nt + t)
    return pl.pallas_call(
        _out_kernel,
        grid=(b, nt),
        in_specs=[pl.BlockSpec((None, TB, d), lambda i, t: (i, t, 0)),
                  pl.BlockSpec((None, TB, c), lambda i, t: (i, t + 1, 0)),
                  pl.BlockSpec((None, TB, c), lambda i, t: (i, t + 1, 0)),
                  pl.BlockSpec((TB, c), lambda i, t: (t, i)),
                  vec, vec, mod, mod, mod,
                  pl.BlockSpec((c, d), const2),
                  pl.BlockSpec((c, d), const2),
                  pl.BlockSpec((1, c), const2),
                  vec, vec,
                  pl.BlockSpec((ne, d), const2),
                  pl.BlockSpec((ne, 1), const2),
                  pl.BlockSpec((TB, TB), const2),
                  pl.BlockSpec((TB, 128), const2)],
        out_specs=[pl.BlockSpec((None, TB, d), lambda i, t: (i, t, 0)),
                   pl.BlockSpec((TB * (d // LANES), LANES), lambda i, t: (i * nt + t, 0)),
                   pl.BlockSpec((TOP_K, TB), tok),
                   pl.BlockSpec((TOP_K, TB), tok),
                   pl.BlockSpec((TOP_K, TB), tok),
                   pl.BlockSpec((None, ne, 128), lambda i, t: (i * nt + t, 0, 0))],
        out_shape=[jax.ShapeDtypeStruct((b, seq, d), F32),
                   jax.ShapeDtypeStruct((b * seq * (d // LANES), LANES), F32),
                   jax.ShapeDtypeStruct((TOP_K, b * seq), jnp.int32),
                   jax.ShapeDtypeStruct((TOP_K, b * seq), F32),
                   jax.ShapeDtypeStruct((TOP_K, b * seq), jnp.int32),
                   jax.ShapeDtypeStruct((ntile, ne, 128), F32)],
        compiler_params=_cp(("parallel", "arbitrary")),
        name="outproj",
    )(x, o0, o1, fno, ln0_g, ln0_b, gt1, sc2, sh2, w_top, w_bot, b_fno, ln1_g, ln1_b, w_rt, b_r, tri, ones_col)


def _slot_kernel(idx_ref, rank_ref, base_ref, o_ref):
    idx = idx_ref[...]
    base = base_ref[...]
    rowid = lax.broadcasted_iota(jnp.int32, (base.shape[0], idx.shape[1]), 0)
    rows = [jnp.sum(jnp.where(rowid == idx[k:k + 1], base, 0), axis=0, keepdims=True) for k in range(TOP_K)]
    o_ref[...] = rank_ref[...] + jnp.concatenate(rows, 0)


def _slots(idx, rank, base):
    ntile, ne, _ = base.shape
    return pl.pallas_call(
        _slot_kernel,
        grid=(ntile,),
        in_specs=[pl.BlockSpec((TOP_K, TB), lambda i: (0, i)),
                  pl.BlockSpec((TOP_K, TB), lambda i: (0, i)),
                  pl.BlockSpec((None, ne, 1), lambda i: (i, 0, 0))],
        out_specs=pl.BlockSpec((None, TOP_K, TB), lambda i: (i, 0, 0)),
        out_shape=jax.ShapeDtypeStruct((ntile, TOP_K, TB), jnp.int32),
        compiler_params=_cp(("parallel",)),
        name="slot",
    )(idx, rank, base)


def _row_copy(src, src_row, dst, dst_row, sem, pieces):
    return pltpu.make_async_copy(src.at[pl.ds(pl.multiple_of(src_row * pieces, pieces), pieces)],
                                 dst.at[pl.ds(pl.multiple_of(dst_row * pieces, pieces), pieces)], sem)


def _slot_table(slot_hbm, slot_smem, ssem):
    i = pl.program_id(0)
    cur = i % 2

    def copy(tile, buf):
        return pltpu.make_async_copy(slot_hbm.at[tile], slot_smem.at[buf], ssem.at[buf])

    @pl.when(i == 0)
    def _():
        copy(0, 0).start()

    @pl.when(i + 1 < pl.num_programs(0))
    def _():
        copy(i + 1, 1 - cur).start()

    copy(i, cur).wait()
    return cur


def _dispatch_kernel(pad_ref, slot_hbm, h_ref, xs_ref, slot_smem, zrow_ref, sem, ssem):
    i = pl.program_id(0)
    pieces = zrow_ref.shape[0]
    cur = _slot_table(slot_hbm, slot_smem, ssem)

    def issue(j, carry):
        for k in range(TOP_K):
            _row_copy(h_ref, j, xs_ref, slot_smem[cur, k, j], sem, pieces).start(priority=k % 2)
        return carry

    lax.fori_loop(0, TB, issue, 0)

    def drain(j, carry):
        for k in range(TOP_K):
            _row_copy(h_ref, 0, xs_ref, 0, sem, pieces).wait()
        return carry

    lax.fori_loop(0, TB, drain, 0)

    @pl.when(i == pl.num_programs(0) - 1)
    def _():
        zrow_ref[...] = jnp.zeros_like(zrow_ref)

        def per_expert(e, carry):
            start = pad_ref[e]
            n = pad_ref[N_EXPERTS + e]

            def zi(q, c2):
                _row_copy(zrow_ref, 0, xs_ref, start + q, sem, pieces).start()
                return c2

            lax.fori_loop(0, n, zi, 0)

            def zw(q, c2):
                _row_copy(zrow_ref, 0, xs_ref, 0, sem, pieces).wait()
                return c2

            lax.fori_loop(0, n, zw, 0)
            return carry

        lax.fori_loop(0, N_EXPERTS, per_expert, 0)


def _dispatch(padinfo, slot, h2, n_slots, pieces):
    ntile = h2.shape[0] // (TB * pieces)
    return pl.pallas_call(
        _dispatch_kernel,
        grid_spec=pltpu.PrefetchScalarGridSpec(
            num_scalar_prefetch=1,
            grid=(ntile,),
            in_specs=[pl.BlockSpec(memory_space=pl.ANY),
                      pl.BlockSpec((TB * pieces, LANES), lambda i, pad: (i, 0))],
            out_specs=pl.BlockSpec(memory_space=pl.ANY),
            scratch_shapes=[pltpu.SMEM((2, TOP_K, TB), jnp.int32),
                            pltpu.VMEM((pieces, LANES), F32),
                            pltpu.SemaphoreType.DMA,
                            pltpu.SemaphoreType.DMA((2,))]),
        out_shape=jax.ShapeDtypeStruct((n_slots * pieces, LANES), F32),
        compiler_params=_cp(("arbitrary",)),
        name="dispatch",
    )(padinfo, slot, h2)


def _expert_kernel(be_ref, nu_ref, x_ref, w1_ref, b1_ref, w2_ref, b2_ref, perm_ref, o_ref, w1b_ref, w2b_ref):
    i = pl.program_id(0)
    prev = be_ref[jnp.maximum(i - 1, 0)]
    changed = jnp.logical_or(i == 0, be_ref[i] != prev)
    dff2 = w1_ref.shape[1]
    nblk = dff2 // 256

    @pl.when(changed)
    def _():
        perm = perm_ref[...]
        for j in range(nblk):
            cs = slice(j * 256, (j + 1) * 256)
            w1b_ref[:, cs] = _dot(w1_ref[:, cs].astype(BF16), perm).astype(BF16)
        w2b_ref[...] = w2_ref[...].astype(BF16)

    @pl.when(i < nu_ref[0])
    def _():
        x = _load_token_rows(x_ref, EBLK, w1_ref.shape[0] // LANES)
        u = _dot(x.astype(BF16), w1b_ref[...]) + b1_ref[...]
        acts = []
        for j in range(nblk):
            glu = jnp.minimum(u[:, j * 256:j * 256 + 128], SWIGLU_LIMIT)
            lin = jnp.clip(u[:, j * 256 + 128:(j + 1) * 256], -SWIGLU_LIMIT, SWIGLU_LIMIT)
            acts.append(glu * _sigmoid(SWIGLU_ALPHA * glu) * (lin + 1.0))
        act = jnp.concatenate(acts, 1).astype(BF16)
        _store_token_rows(o_ref, _dot(act, w2b_ref[...]) + b2_ref[...])

    @pl.when(i >= nu_ref[0])
    def _():
        o_ref[...] = jnp.zeros_like(o_ref)


def _experts(block_e, n_used, xs, w1, b1p, w2, b2, perm):
    ne, d, dff2 = w1.shape
    dff = w2.shape[1]
    pieces = d // LANES
    nblocks = xs.shape[0] // (EBLK * pieces)
    return pl.pallas_call(
        _expert_kernel,
        grid_spec=pltpu.PrefetchScalarGridSpec(
            num_scalar_prefetch=2,
            grid=(nblocks,),
            in_specs=[pl.BlockSpec((EBLK * pieces, LANES), lambda i, be, nu: (jnp.minimum(i, nu[0] - 1), 0)),
                      pl.BlockSpec((None, d, dff2), lambda i, be, nu: (be[i], 0, 0)),
                      pl.BlockSpec((None, 1, dff2), lambda i, be, nu: (be[i], 0, 0)),
                      pl.BlockSpec((None, dff, d), lambda i, be, nu: (be[i], 0, 0)),
                      pl.BlockSpec((None, 1, d), lambda i, be, nu: (be[i], 0, 0)),
                      pl.BlockSpec((256, 256), lambda i, be, nu: (0, 0))],
            out_specs=pl.BlockSpec((EBLK * pieces, LANES), lambda i, be, nu: (i, 0)),
            scratch_shapes=[pltpu.VMEM((d, dff2), BF16),
                            pltpu.VMEM((dff, d), BF16)]),
        out_shape=jax.ShapeDtypeStruct(xs.shape, F32),
        compiler_params=_cp(("arbitrary",)),
        name="experts",
    )(block_e, n_used, xs, w1, b1p, w2, b2, perm)


def _combine_kernel(slot_hbm, ys_ref, gate_ref, x1_ref, gt_ref, g_ref, b_ref, eye_ref, o_ref, slot_smem, buf_ref, sem, ssem):
    s = pl.program_id(0)
    n = pl.num_programs(0)
    pieces = o_ref.shape[1] // LANES
    cur = s % 2
    nxt = 1 - cur

    def table(tile, b):
        return pltpu.make_async_copy(slot_hbm.at[tile], slot_smem.at[b], ssem.at[b])

    def gather(b):
        def issue(j, carry):
            for k in range(TOP_K):
                _row_copy(ys_ref, slot_smem[b, k, j], buf_ref.at[b, k], j, sem.at[b], pieces).start(priority=k % 2)
            return carry

        lax.fori_loop(0, TB, issue, 0)

    @pl.when(s == 0)
    def _():
        table(0, 0).start()
        table(0, 0).wait()
        gather(0)

        @pl.when(n > 1)
        def _():
            table(1, 1).start()

    @pl.when(s + 1 < n)
    def _():
        table(s + 1, nxt).wait()
        gather(nxt)

        @pl.when(s + 2 < n)
        def _():
            table(s + 2, cur).start()

    def drain(j, carry):
        for k in range(TOP_K):
            _row_copy(ys_ref, 0, buf_ref.at[cur, k], 0, sem.at[cur], pieces).wait()
        return carry

    lax.fori_loop(0, TB, drain, 0)

    gt = _dot_nt(eye_ref[...], gate_ref[...], precision=HIGHEST)
    f = gt[:, 0:1] * _load_token_rows(buf_ref.at[cur, 0], TB, pieces)
    for k in range(1, TOP_K):
        f = f + gt[:, k:k + 1] * _load_token_rows(buf_ref.at[cur, k], TB, pieces)
    o_ref[...] = _ln(DEEPNORM_ALPHA * x1_ref[...] + gt_ref[...] * f, g_ref[...], b_ref[...])


def _combine(slot, ys, gates, x1, gt2, ln_g, ln_b, eye, tiles_per_batch):
    n, d = x1.shape
    ntile = n // TB
    return pl.pallas_call(
        _combine_kernel,
        grid=(ntile,),
        in_specs=[pl.BlockSpec(memory_space=pl.ANY),
                  pl.BlockSpec(memory_space=pl.ANY),
                  pl.BlockSpec((TOP_K, TB), lambda i: (0, i)),
                  pl.BlockSpec((TB, d), lambda i: (i, 0)),
                  pl.BlockSpec((None, 1, d), lambda i: (i // tiles_per_batch, 0, 0)),
                  pl.BlockSpec((1, d), lambda i: (0, 0)),
                  pl.BlockSpec((1, d), lambda i: (0, 0)),
                  pl.BlockSpec((TB, TB), lambda i: (0, 0))],
        out_specs=pl.BlockSpec((TB, d), lambda i: (i, 0)),
        out_shape=jax.ShapeDtypeStruct((n, d), F32),
        scratch_shapes=[pltpu.SMEM((2, TOP_K, TB), jnp.int32),
                        pltpu.VMEM((2, TOP_K, TB * (d // LANES), LANES), F32),
                        pltpu.SemaphoreType.DMA((2,)),
                        pltpu.SemaphoreType.DMA((2,))],
        compiler_params=_cp(("arbitrary",)),
        name="combine",
    )(slot, ys, gates, x1, gt2, ln_g, ln_b, eye)


def _scan_tables(reverse):
    t = np.arange(TB)
    same = (t[:, None] // CHUNK) == (t[None, :] // CHUNK)
    tri = (t[None, :] >= t[:, None]) if reverse else (t[None, :] <= t[:, None])
    return jnp.asarray(np.concatenate([same & tri, same], 0).astype(np.float32), BF16)


def _head_ones(c):
    h = np.arange(c) // HEAD
    return jnp.asarray((h[:, None] == h[None, :]).astype(np.float32), BF16)


def _deinterleave_perm():
    p = np.zeros((256, 256), np.float32)
    j = np.arange(128)
    p[2 * j, j] = 1.0
    p[2 * j + 1, 128 + j] = 1.0
    return jnp.asarray(p, BF16)


def kernel(x, c, ctx, c_ctx, ln0_g, ln0_b, w_ada, b_ada, w_in, mu_shift, w0, w2_decay, a0, a2_iclr, g2_gate, r_k, k_k, k_a, gn_g, gn_b, w_fno, b_fno, w_out, ln1_g, ln1_b, w_router, b_router, w1, b1, w2, b2, ln2_g, ln2_b):
    b, seq, d = x.shape
    ctx_len = ctx.shape[1]
    assert ctx_len == TB and seq % TB == 0 and w_ada.shape[0] == 1
    n_dir, cw = w0.shape[1], w0.shape[2]
    fgroups, gw = w_fno.shape[1], w_fno.shape[2]
    fw = fgroups * gw
    shift_w = mu_shift.shape[1]
    nd, na, ng = w2_decay.shape[2], a2_iclr.shape[2], g2_gate.shape[2]
    assert n_dir == 2 and shift_w == 3 * cw + 2 * (nd + na + ng)
    ne = w_router.shape[2]
    n_tok = b * seq
    row = lambda a: a.reshape(1, -1)

    rows = -(-(b + 1) // 8) * 8
    cc = jnp.zeros((rows, d), F32).at[:b].set(c).at[b].set(c_ctx)
    mod = _ada(cc, w_ada[0], b_ada[0])
    sh1, sc1, gt1, sh2, sc2, gt2 = [mod[:b, i * d:(i + 1) * d] for i in range(6)]
    sh1c, sc1c = mod[b, :d], mod[b, d:2 * d]
    scsel = jnp.stack([jnp.broadcast_to(sc1c, (b, d)), sc1], 1)[:, :, None, :]
    shsel = jnp.stack([jnp.broadcast_to(sh1c, (b, d)), sh1], 1)[:, :, None, :]
    mod3 = lambda a: a[:, None, :]

    cidx = np.arange(gw)
    ang = 2.0 * np.pi * ((cidx[:, None] * cidx[None, :]) % gw) / gw
    norm = 1.0 / math.sqrt(seq * gw)
    csc = jnp.asarray(np.stack([np.cos(ang) * norm, -np.sin(ang) * norm]), F32)
    win_f3 = w_in[0][:, shift_w:].reshape(d, fgroups, gw).transpose(1, 0, 2)
    wfc, wfs = _fold(csc, w_fno[0], win_f3)
    unf = lambda a: a.transpose(1, 0, 2).reshape(d, fw)
    w3 = jnp.concatenate([w_in[0][:, :shift_w], unf(wfc), unf(wfs)], 1).astype(BF16)

    s_all, fcat = _inproj(ctx, x, scsel, shsel, row(ln0_g), row(ln0_b), w3, shift_w, fw)

    def pad_dir(w, width):
        out = jnp.zeros((2, 2 * width, cw), F32)
        return out.at[0, :width].set(w[0]).at[1, width:].set(w[1]).astype(BF16)

    ones_bd = _head_ones(cw)
    r, v, kk, lw, kd, bb, g = _prep(
        s_all, row(mu_shift[0]), w0[0][:, None, :], pad_dir(w2_decay[0], nd), a0[0][:, None, :],
        pad_dir(a2_iclr[0], na), pad_dir(g2_gate[0], ng), row(k_k[0]), row(k_a[0]), ones_bd, cw)

    rk = r_k[0].reshape(2, 1, cw)
    outs = [_scan(r, v, kk, lw, kd, bb, g, rk, row(gn_g[0]), row(gn_b[0]), _scan_tables(dr == 1), ones_bd, dr)
            for dr in range(2)]

    fno = _matmul(_dft_matrix(seq), fcat.reshape(2 * seq, b * fw), 1024, 2048, 1024)

    ids = np.arange(TB)
    tri = jnp.asarray((ids[:, None] < ids[None, :]).astype(np.float32), BF16)
    ones_col = jnp.ones((TB, 128), BF16)
    wo = w_out[0].astype(BF16)
    x1, h2, idx, gates, rank, cnt = _outproj(
        x, outs[0], outs[1], fno, row(ln0_g), row(ln0_b), mod3(gt1), mod3(sc2), mod3(sh2), wo[:cw], wo[cw:],
        row(b_fno[0]), row(ln1_g[0]), row(ln1_b[0]), w_router[0].T, b_router[0].reshape(ne, 1), tri, ones_col)

    ntile = n_tok // TB
    cnt_t = cnt[:, :, 0].astype(jnp.int32)
    counts = jnp.sum(cnt_t, 0)
    padded = (counts + EBLK - 1) // EBLK * EBLK
    pends = jnp.cumsum(padded)
    pstarts = pends - padded
    base = (pstarts[None, :] + jnp.cumsum(cnt_t, 0) - cnt_t)[:, :, None]
    nblocks = (n_tok * TOP_K) // EBLK + ne
    n_slots = nblocks * EBLK
    block_start = jnp.arange(nblocks, dtype=jnp.int32) * EBLK
    block_e = jnp.minimum(jnp.sum((pends[None, :] <= block_start[:, None]).astype(jnp.int32), 1), ne - 1)
    n_used = (pends[-1:] // EBLK).astype(jnp.int32)
    padinfo = jnp.concatenate([pstarts + counts, padded - counts]).astype(jnp.int32)

    slot = _slots(idx, rank, base)
    xs = _dispatch(padinfo, slot, h2, n_slots, d // LANES)
    b1p = b1[0].reshape(ne, -1, 128, 2).transpose(0, 1, 3, 2).reshape(ne, 1, -1)
    ys = _experts(block_e, n_used, xs, w1[0], b1p, w2[0], b2[0][:, None, :], _deinterleave_perm())
    out = _combine(slot, ys, gates, x1.reshape(n_tok, d), mod3(gt2), row(ln2_g[0]), row(ln2_b[0]),
                   jnp.eye(TB, dtype=F32), seq // TB)
    return out.reshape(b, seq, d)
```
